```python
import math
import jax
import jax.numpy as jnp
from jax import lax
import numpy as np

D_MODEL = 2048
BATCH = 1
SEQ = 8192
DEPTH = 1
DEC_BATCH = 32
DEC_SEQ = 1
PAST_LEN = 16384
PAGE_SIZE = 128

HEAD_DIM = 64
RW_HEADS = 16
SB_HEADS = 16
RW_WIDTH = RW_HEADS * HEAD_DIM
SB_WIDTH = SB_HEADS * HEAD_DIM
MIX_WIDTH = RW_WIDTH + SB_WIDTH
DECAY_LORA = 64
AAA_LORA = 64
GATE_LORA = 160
RW_COLS = 3 * RW_WIDTH + DECAY_LORA + AAA_LORA + GATE_LORA
IN_COLS = 3 * SB_WIDTH + RW_COLS
GN_EPS = 64e-5
LN_EPS = 1e-5
Q_BLOCK = 128
SB_SCALE = 1.0 / math.sqrt(HEAD_DIM)
SB_BIAS_INIT = -6.0
PEER_HEADS = 8
N_KEYS = 128
N_EXPERTS = N_KEYS * N_KEYS
D_KEY = 256
TOPK = 16
TOK_BLOCK = 128
DN_ALPHA = (2 * DEPTH) ** 0.25
DN_BETA = (8 * DEPTH) ** -0.25

kernel_name = "hymba_rwkv7_stickbreak_peer_step"


def layer_norm(x, g, b):
    xf = x.astype(jnp.float32)
    mu = jnp.mean(xf, axis=-1, keepdims=True)
    var = jnp.mean(jnp.square(xf - mu), axis=-1, keepdims=True)
    return (xf - mu) * lax.rsqrt(var + LN_EPS) * g + b


def _wkv_step(S, inp):
    r, w, k, v, kk, b = inp
    sa = jnp.einsum('bhvk,bhk->bhv', S, -kk)
    S = S * w[:, :, None, :] + sa[..., None] * b[:, :, None, :] + v[..., None] * k[:, :, None, :]
    y = jnp.einsum('bhvk,bhk->bhv', S, r)
    return S, y


def rwkv7_group(p, shift_prev, wkv_prev, lw):
    bsz, t, _ = p.shape
    p = p.astype(jnp.float32)
    prev = jnp.concatenate([shift_prev.astype(jnp.float32), p[:, :-1]], axis=1)
    xs = p + (prev - p) * lw['mu_shift']
    c = np.cumsum([RW_WIDTH, RW_WIDTH, RW_WIDTH, DECAY_LORA, AAA_LORA])
    r, k, v, w_lo, a_lo, g_lo = jnp.split(xs, c.tolist(), axis=-1)
    w = -jax.nn.softplus(-(lw['w0'] + jnp.tanh(w_lo) @ lw['w_decay_up'])) - 0.5
    decay = jnp.exp(-jnp.exp(w))
    iclr = jax.nn.sigmoid(lw['a0'] + a_lo @ lw['w_aaa_up'])
    gate = jax.nn.sigmoid(g_lo) @ lw['w_gate_up']
    heads = lambda u: u.reshape(bsz, t, RW_HEADS, HEAD_DIM)
    kk = heads(k * lw['k_k'])
    kk = kk / jnp.maximum(jnp.linalg.norm(kk, axis=-1, keepdims=True), 1e-12)
    k = k * (1.0 + (iclr - 1.0) * lw['k_a'])
    r, k, v, decay, iclr = heads(r), heads(k), heads(v), heads(decay), heads(iclr)
    tm = lambda u: jnp.moveaxis(u, 1, 0)
    S_T, y = lax.scan(_wkv_step, wkv_prev.astype(jnp.float32),
                      (tm(r), tm(decay), tm(k), tm(v), tm(kk), tm(kk * iclr)))
    y = jnp.moveaxis(y, 0, 1)
    m = jnp.mean(y, axis=-1, keepdims=True)
    var = jnp.mean(jnp.square(y - m), axis=-1, keepdims=True)
    y = ((y - m) * lax.rsqrt(var + GN_EPS)).reshape(bsz, t, RW_WIDTH) * lw['lnx_g'] + lw['lnx_b']
    bonus = jnp.sum(r * k * lw['r_k'], axis=-1, keepdims=True) * v
    out = (y + bonus.reshape(bsz, t, RW_WIDTH)) * gate
    return out, S_T, p[:, -1:]


def stick_breaking_block(q, k, v, q_start, bias):
    tq, tk = q.shape[1], k.shape[1]
    z = jnp.einsum('bqhd,bkhd->bhqk', q, k, preferred_element_type=jnp.float32) * SB_SCALE
    z = z + bias.astype(jnp.float32)[None, :, None, None]
    q_pos = q_start + jnp.arange(tq)
    k_pos = jnp.arange(tk)
    causal = k_pos[None, :] < q_pos[:, None]
    log_keep = jnp.where(causal, jax.nn.log_sigmoid(-z), 0.0)
    later = lax.cumsum(log_keep, axis=3, reverse=True) - log_keep
    att = jnp.where(causal, jnp.exp(jax.nn.log_sigmoid(z) + later), 0.0)
    return jnp.einsum('bhqk,bkhd->bqhd', att.astype(v.dtype), v, preferred_element_type=jnp.float32)


def stick_breaking_sweep(q, k, v, q_offset, bias):
    tq = q.shape[1]
    outs = []
    for start in range(0, tq, Q_BLOCK):
        end = min(start + Q_BLOCK, tq)
        kend = q_offset + end
        outs.append(stick_breaking_block(q[:, start:end], k[:, :kend], v[:, :kend], q_offset + start, bias))
    return jnp.concatenate(outs, axis=1)


def peer_tokens(x, lw):
    t = x.shape[0]
    q = (x @ lw['peer_wq']).reshape(t, PEER_HEADS, 2, D_KEY // 2)
    s = jnp.einsum('thcd,hcnd->thcn', q, lw['peer_subkeys'], preferred_element_type=jnp.float32)
    s1, i1 = lax.top_k(s[:, :, 0], TOPK)
    s2, i2 = lax.top_k(s[:, :, 1], TOPK)
    cand = (s1[..., :, None] + s2[..., None, :]).reshape(t, PEER_HEADS, TOPK * TOPK)
    sc, ci = lax.top_k(cand, TOPK)
    e1 = jnp.take_along_axis(i1, ci // TOPK, axis=-1)
    e2 = jnp.take_along_axis(i2, ci % TOPK, axis=-1)
    expert = e1 * N_KEYS + e2
    gate = jax.nn.softmax(sc, axis=-1)
    u = lw['peer_u'][expert]
    act = jax.nn.gelu(jnp.einsum('thkd,td->thk', u, x, preferred_element_type=jnp.float32), approximate=False)
    coef = (gate * act).astype(x.dtype)
    return jnp.einsum('thk,thkd->td', coef, lw['peer_v'][expert], preferred_element_type=jnp.float32)


def peer_ffn(x, lw):
    bsz, t, d = x.shape
    flat = x.reshape(bsz * t, d)
    n = flat.shape[0]
    if n % TOK_BLOCK == 0 and n > TOK_BLOCK:
        out = lax.map(lambda xb: peer_tokens(xb, lw), flat.reshape(n // TOK_BLOCK, TOK_BLOCK, d))
    else:
        out = peer_tokens(flat, lw)
    return out.reshape(bsz, t, d)


def hybrid_layer(x, shift_prev, wkv_prev, past_k, past_v, lw):
    bsz, t, _ = x.shape
    p = x @ lw['w_in']
    q_sb = p[..., :SB_WIDTH].reshape(bsz, t, SB_HEADS, HEAD_DIM)
    k_sb = p[..., SB_WIDTH:2 * SB_WIDTH].reshape(bsz, t, SB_HEADS, HEAD_DIM)
    v_sb = p[..., 2 * SB_WIDTH:3 * SB_WIDTH].reshape(bsz, t, SB_HEADS, HEAD_DIM)
    p_rw = p[..., 3 * SB_WIDTH:]
    if past_k is None:
        k_all, v_all, offset = k_sb, v_sb, 0
    else:
        k_all = jnp.concatenate([past_k.astype(k_sb.dtype), k_sb], axis=1)
        v_all = jnp.concatenate([past_v.astype(v_sb.dtype), v_sb], axis=1)
        offset = past_k.shape[1]
    sb_out = stick_breaking_sweep(q_sb, k_all, v_all, offset, lw['sb_bias']).reshape(bsz, t, SB_WIDTH)
    rw_out, wkv_new, shift_new = rwkv7_group(p_rw, shift_prev, wkv_prev, lw)
    mix = jnp.concatenate([rw_out.astype(x.dtype), sb_out.astype(x.dtype)], axis=-1) @ lw['w_out']
    h = layer_norm(DN_ALPHA * x + mix, lw['ln1_g'], lw['ln1_b']).astype(x.dtype)
    y = layer_norm(DN_ALPHA * h + peer_ffn(h, lw).astype(h.dtype), lw['ln2_g'], lw['ln2_b']).astype(x.dtype)
    return y, k_sb, v_sb, wkv_new, shift_new


def setup_inputs(seed: int = 0) -> dict:
    key = jax.random.key(seed)
    ks = iter(jax.random.split(key, 48))
    nrm = lambda shape, scale: jax.random.normal(next(ks), shape, jnp.float32) * scale
    uni = lambda shape, lo, hi: jax.random.uniform(next(ks), shape, jnp.float32, lo, hi)
    n_pages = PAST_LEN // PAGE_SIZE
    n_used = DEC_BATCH * n_pages
    n_pool = n_used + max(1, n_used // 4)
    page_table = jax.random.permutation(next(ks), n_pool)[:n_used].reshape(DEC_BATCH, n_pages).astype(jnp.int32)
    L = DEPTH
    return {
        'x_prompt': nrm((BATCH, SEQ, D_MODEL), 1.0),
        'x_sample': nrm((DEC_BATCH, DEC_SEQ, D_MODEL), 1.0),
        'cache_sb_k': nrm((L, n_pool, PAGE_SIZE, SB_HEADS, HEAD_DIM), 1.0),
        'cache_sb_v': nrm((L, n_pool, PAGE_SIZE, SB_HEADS, HEAD_DIM), 1.0),
        'state_wkv': nrm((L, DEC_BATCH, RW_HEADS, HEAD_DIM, HEAD_DIM), 0.3),
        'state_shift': nrm((L, DEC_BATCH, 1, RW_COLS), 1.0),
        'page_table': page_table,
        'w_in': nrm((L, D_MODEL, IN_COLS), D_MODEL ** -0.5),
        'sb_bias': SB_BIAS_INIT + nrm((L, SB_HEADS), 0.1),
        'mu_shift': uni((L, RW_COLS), 0.0, 1.0),
        'w0': uni((L, RW_WIDTH), -7.0, 1.0),
        'w_decay_up': nrm((L, DECAY_LORA, RW_WIDTH), 0.1),
        'a0': nrm((L, RW_WIDTH), 0.3),
        'w_aaa_up': nrm((L, AAA_LORA, RW_WIDTH), 0.1),
        'w_gate_up': nrm((L, GATE_LORA, RW_WIDTH), GATE_LORA ** -0.5),
        'k_k': 0.85 + nrm((L, RW_WIDTH), 0.05),
        'k_a': 1.0 + nrm((L, RW_WIDTH), 0.05),
        'r_k': nrm((L, RW_HEADS, HEAD_DIM), 0.1),
        'lnx_g': 1.0 + nrm((L, RW_WIDTH), 0.02),
        'lnx_b': nrm((L, RW_WIDTH), 0.02),
        'w_out': nrm((L, MIX_WIDTH, D_MODEL), DN_BETA * MIX_WIDTH ** -0.5),
        'ln1_g': 1.0 + nrm((L, D_MODEL), 0.02),
        'ln1_b': nrm((L, D_MODEL), 0.02),
        'peer_wq': nrm((L, D_MODEL, PEER_HEADS * D_KEY), D_MODEL ** -0.5),
        'peer_subkeys': nrm((L, PEER_HEADS, 2, N_KEYS, D_KEY // 2), (D_KEY // 2) ** -0.5),
        'peer_u': nrm((L, N_EXPERTS, D_MODEL), D_MODEL ** -0.5),
        'peer_v': nrm((L, N_EXPERTS, D_MODEL), DN_BETA * PEER_HEADS ** -0.5),
        'ln2_g': 1.0 + nrm((L, D_MODEL), 0.02),
        'ln2_b': nrm((L, D_MODEL), 0.02),
    }


def reference(x_prompt, x_sample, cache_sb_k, cache_sb_v, state_wkv, state_shift, page_table,
              w_in, sb_bias, mu_shift, w0, w_decay_up, a0, w_aaa_up, w_gate_up, k_k, k_a, r_k, lnx_g, lnx_b,
              w_out, ln1_g, ln1_b, peer_wq, peer_subkeys, peer_u, peer_v, ln2_g, ln2_b):
    dec_b, n_pages = page_table.shape
    past_len = n_pages * cache_sb_k.shape[2]
    bsz = x_prompt.shape[0]
    h_p, h_s = x_prompt, x_sample
    kp, vp, wp, sp, ksm, vsm, wsm, ssm = [], [], [], [], [], [], [], []
    for l in range(DEPTH):
        lw = dict(w_in=w_in[l], sb_bias=sb_bias[l], mu_shift=mu_shift[l], w0=w0[l], w_decay_up=w_decay_up[l],
                  a0=a0[l], w_aaa_up=w_aaa_up[l], w_gate_up=w_gate_up[l], k_k=k_k[l], k_a=k_a[l], r_k=r_k[l],
                  lnx_g=lnx_g[l], lnx_b=lnx_b[l], w_out=w_out[l], ln1_g=ln1_g[l], ln1_b=ln1_b[l],
                  peer_wq=peer_wq[l], peer_subkeys=peer_subkeys[l], peer_u=peer_u[l], peer_v=peer_v[l],
                  ln2_g=ln2_g[l], ln2_b=ln2_b[l])
        zero_shift = jnp.zeros((bsz, 1, RW_COLS), jnp.float32)
        zero_wkv = jnp.zeros((bsz, RW_HEADS, HEAD_DIM, HEAD_DIM), jnp.float32)
        h_p, k_new, v_new, wkv_new, shift_new = hybrid_layer(h_p, zero_shift, zero_wkv, None, None, lw)
        kp.append(k_new); vp.append(v_new); wp.append(wkv_new); sp.append(shift_new)
        past_k = cache_sb_k[l][page_table].reshape(dec_b, past_len, SB_HEADS, HEAD_DIM)
        past_v = cache_sb_v[l][page_table].reshape(dec_b, past_len, SB_HEADS, HEAD_DIM)
        h_s, k_new, v_new, wkv_new, shift_new = hybrid_layer(h_s, state_shift[l], state_wkv[l], past_k, past_v, lw)
        ksm.append(k_new); vsm.append(v_new); wsm.append(wkv_new); ssm.append(shift_new)
    k_prompt, v_prompt = jnp.stack(kp), jnp.stack(vp)
    wkv_prompt, shift_prompt = jnp.stack(wp), jnp.stack(sp)
    k_sample, v_sample = jnp.stack(ksm), jnp.stack(vsm)
    wkv_sample, shift_sample = jnp.stack(wsm), jnp.stack(ssm)
    return (h_p, h_s, k_prompt, v_prompt, wkv_prompt, shift_prompt, k_sample, v_sample, wkv_sample, shift_sample)
```

```python
import functools
import math

import numpy as np
import jax
import jax.numpy as jnp
from jax import lax
from jax.experimental import pallas as pl
from jax.experimental.pallas import tpu as pltpu

F32 = jnp.float32
BF16 = jnp.bfloat16

LANES = 128
HEAD_DIM = 64
SEG = 256
GN_EPS = 64e-5
LN_EPS = 1e-5
TOPK = 16
NEG_INF = float("-inf")
VMEM_LIMIT_BYTES = 56 * 1024 * 1024


def _params(*semantics):
    return pltpu.CompilerParams(dimension_semantics=semantics, vmem_limit_bytes=VMEM_LIMIT_BYTES)


def _head_ones():
    blocks = SEG // HEAD_DIM
    return jnp.asarray(np.kron(np.eye(blocks), np.ones((HEAD_DIM, HEAD_DIM))), dtype=BF16)


def _split_bf16(x):
    hi = x.astype(BF16)
    lo = (x - hi.astype(F32)).astype(BF16)
    return hi, lo


def _head_sum(x, ones):
    outs = []
    for g in range(x.shape[-1] // SEG):
        hi, lo = _split_bf16(x[:, g * SEG:(g + 1) * SEG])
        outs.append(jnp.dot(hi, ones, preferred_element_type=F32)
                    + jnp.dot(lo, ones, preferred_element_type=F32))
    return outs[0] if len(outs) == 1 else jnp.concatenate(outs, axis=-1)


def _softplus(x):
    return jnp.maximum(x, 0.0) + jnp.log1p(jnp.exp(-jnp.abs(x)))


def _layer_norm(x, g, b):
    mu = jnp.mean(x, axis=-1, keepdims=True)
    xc = x - mu
    var = jnp.mean(xc * xc, axis=-1, keepdims=True)
    return xc * lax.rsqrt(var + LN_EPS) * g + b


def _matmul_kernel(x_ref, w_ref, o_ref):
    o_ref[...] = jnp.dot(x_ref[...].astype(BF16), w_ref[...], preferred_element_type=F32)


def _matmul(x, w, tm, tn, name):
    m, k = x.shape
    n = w.shape[1]
    assert m % tm == 0 and n % tn == 0
    return pl.pallas_call(
        _matmul_kernel,
        grid=(n // tn, m // tm),
        in_specs=[pl.BlockSpec((tm, k), lambda j, i: (i, 0)),
                  pl.BlockSpec((k, tn), lambda j, i: (0, j))],
        out_specs=pl.BlockSpec((tm, tn), lambda j, i: (i, j)),
        out_shape=jax.ShapeDtypeStruct((m, n), F32),
        compiler_params=_params("parallel", "parallel"),
        name=name,
    )(x, w)


def _rw_prep_kernel(halo, *refs):
    if halo:
        (pr, pk, pv, pt, hr, hk, hv, ht, sr, sk, sv, st,
         mur, muk, muv, mut, w0, a0, kk_s, ka_s, rk_s, wd, wa, wg, ones_ref,
         r_o, w_o, k_o, v_o, a_o, b_o, g_o, bon_o) = refs
    else:
        (pr, pk, pv, pt, qr, qk, qv, qt,
         mur, muk, muv, mut, w0, a0, kk_s, ka_s, rk_s, wd, wa, wg, ones_ref,
         r_o, w_o, k_o, v_o, a_o, b_o, g_o, bon_o) = refs

    def shifted(x_ref, idx):
        x = x_ref[...]
        if not halo:
            return x, (qr, qk, qv, qt)[idx][...]
        h_ref = (hr, hk, hv, ht)[idx]
        s_ref = (sr, sk, sv, st)[idx]
        first = jnp.where(pl.program_id(0) == 0, s_ref[...], h_ref[7:8, :])
        row = lax.broadcasted_iota(jnp.int32, x.shape, 0)
        return x, jnp.where(row == 0, first, pltpu.roll(x, 1, axis=0))

    def lerp(x_ref, idx, mu_ref):
        x, prev = shifted(x_ref, idx)
        return x + (prev - x) * mu_ref[...]

    ones = ones_ref[...]
    xr = lerp(pr, 0, mur)
    xk = lerp(pk, 1, muk)
    xv = lerp(pv, 2, muv)
    xt = lerp(pt, 3, mut)

    dec_in = jnp.dot(jnp.tanh(xt).astype(BF16), wd[...], preferred_element_type=F32)
    w = -_softplus(-(w0[...] + dec_in)) - 0.5
    decay = jnp.exp(-jnp.exp(w))
    iclr = jax.nn.sigmoid(a0[...] + jnp.dot(xt.astype(BF16), wa[...], preferred_element_type=F32))
    gate = jnp.dot(jax.nn.sigmoid(xt).astype(BF16), wg[...], preferred_element_type=F32)

    kk = xk * kk_s[...]
    norm = jnp.sqrt(_head_sum(kk * kk, ones))
    kk = kk / jnp.maximum(norm, 1e-12)
    k2 = xk * (1.0 + (iclr - 1.0) * ka_s[...])
    bonus = _head_sum(xr * k2 * rk_s[...], ones) * xv

    r_o[...] = xr
    w_o[...] = decay
    k_o[...] = k2
    v_o[...] = xv
    a_o[...] = -kk
    b_o[...] = kk * iclr
    g_o[...] = gate
    bon_o[...] = bonus


def _rw_prep(p, prev, shift_rows, prm, tm, rw_width, sb_cols, tail_w):
    m = p.shape[0]
    assert m % tm == 0
    cb = sb_cols // rw_width
    tb = (sb_cols + 3 * rw_width) // tail_w
    assert cb * rw_width == sb_cols and tb * tail_w == sb_cols + 3 * rw_width
    halo = prev is None

    def col(width, c):
        return pl.BlockSpec((tm, width), lambda i, c=c: (i, c))

    def halo_col(width, c):
        return pl.BlockSpec((8, width), lambda i, c=c: (jnp.maximum(i * (tm // 8) - 1, 0), c))

    def row(width):
        return pl.BlockSpec((1, width), lambda i: (0, 0))

    def whole(shape):
        return pl.BlockSpec(shape, lambda i: (0,) * len(shape))

    main_specs = [col(rw_width, cb), col(rw_width, cb + 1), col(rw_width, cb + 2), col(tail_w, tb)]
    if halo:
        assert tm % 8 == 0
        extra_specs = [halo_col(rw_width, cb), halo_col(rw_width, cb + 1), halo_col(rw_width, cb + 2),
                       halo_col(tail_w, tb), row(rw_width), row(rw_width), row(rw_width), row(tail_w)]
        extra_args = [p, p, p, p] + list(shift_rows)
    else:
        extra_specs = [pl.BlockSpec((tm, rw_width), lambda i: (i, 0))] * 3 + [pl.BlockSpec((tm, tail_w), lambda i: (i, 0))]
        extra_args = list(prev)
    prm_specs = [row(rw_width), row(rw_width), row(rw_width), row(tail_w),
                 row(rw_width), row(rw_width), row(rw_width), row(rw_width), row(rw_width),
                 whole((tail_w, rw_width)), whole((tail_w, rw_width)), whole((tail_w, rw_width)),
                 whole((SEG, SEG))]
    out_spec = pl.BlockSpec((tm, rw_width), lambda i: (i, 0))
    out_shape = jax.ShapeDtypeStruct((m, rw_width), F32)
    return pl.pallas_call(
        functools.partial(_rw_prep_kernel, halo),
        grid=(m // tm,),
        in_specs=main_specs + extra_specs + prm_specs,
        out_specs=[out_spec] * 8,
        out_shape=[out_shape] * 8,
        compiler_params=_params("parallel"),
        name="rw_prep_seq" if halo else "rw_prep_tok",
    )(p, p, p, p, *extra_args, *prm)


def _rw_scan_kernel(r_ref, w_ref, k_ref, v_ref, a_ref, b_ref, s0_ref, ones_ref,
                    y_ref, sT_ref, s_ref):
    tc = r_ref.shape[0]
    width = r_ref.shape[1]

    @pl.when(pl.program_id(1) == 0)
    def _():
        s_ref[...] = s0_ref[...]

    ones = ones_ref[...]
    lane = lax.broadcasted_iota(jnp.int32, (HEAD_DIM, SEG), 1)
    sub = lax.broadcasted_iota(jnp.int32, (HEAD_DIM, SEG), 0)
    diag = (lane % HEAD_DIM) == sub

    def step(t, carry):
        r = r_ref[pl.ds(t, 1), :]
        w = w_ref[pl.ds(t, 1), :]
        k = k_ref[pl.ds(t, 1), :]
        v = v_ref[pl.ds(t, 1), :]
        a = a_ref[pl.ds(t, 1), :]
        b = b_ref[pl.ds(t, 1), :]
        v_hi = v.astype(BF16).astype(F32)
        v_lo = v - v_hi
        ys = []
        for g in range(width // SEG):
            sl = slice(g * SEG, (g + 1) * SEG)
            s = s_ref[:, sl]
            lhs = jnp.concatenate([
                (s * a[:, sl]).astype(BF16),
                jnp.where(diag, v_hi[:, sl], 0.0).astype(BF16),
                jnp.where(diag, v_lo[:, sl], 0.0).astype(BF16)], axis=0)
            res = jnp.dot(lhs, ones, preferred_element_type=F32)
            sa = res[0:HEAD_DIM]
            vb = res[HEAD_DIM:2 * HEAD_DIM] + res[2 * HEAD_DIM:3 * HEAD_DIM]
            s_new = s * w[:, sl] + sa * b[:, sl] + vb * k[:, sl]
            s_ref[:, sl] = s_new
            yb = jnp.dot((s_new * r[:, sl]).astype(BF16), ones, preferred_element_type=F32)
            ys.append(jnp.sum(jnp.where(diag, yb, 0.0), axis=0, keepdims=True))
        y_ref[pl.ds(t, 1), :] = jnp.concatenate(ys, axis=-1)
        return carry

    lax.fori_loop(0, tc, step, 0, unroll=2 if tc % 2 == 0 else 1)

    @pl.when(pl.program_id(1) == pl.num_programs(1) - 1)
    def _():
        sT_ref[...] = s_ref[...]


def _rw_scan(seqs, s0, tc):
    bsz, t, width = seqs[0].shape
    assert t % tc == 0 and width % SEG == 0
    seq_spec = pl.BlockSpec((None, tc, width), lambda b, c: (b, c, 0))
    st_spec = pl.BlockSpec((None, HEAD_DIM, width), lambda b, c: (b, 0, 0))
    return pl.pallas_call(
        _rw_scan_kernel,
        grid=(bsz, t // tc),
        in_specs=[seq_spec] * 6 + [st_spec, pl.BlockSpec((SEG, SEG), lambda b, c: (0, 0))],
        out_specs=[seq_spec, st_spec],
        out_shape=[jax.ShapeDtypeStruct((bsz, t, width), F32),
                   jax.ShapeDtypeStruct((bsz, HEAD_DIM, width), F32)],
        scratch_shapes=[pltpu.VMEM((HEAD_DIM, width), F32)],
        compiler_params=_params("parallel", "arbitrary"),
        name="rw_scan",
    )(*seqs, s0, _head_ones())


def _sb_seq_kernel(scale, bias_ref, q_ref, k_ref, v_ref, tri_ref, o_ref, qs_ref, acc_ref, run_ref):
    tq = q_ref.shape[0]
    tk = tq
    hp = pl.program_id(0)
    qi = pl.program_id(1)
    lane = lax.broadcasted_iota(jnp.int32, (tq, LANES), 1)
    q = q_ref[...] * scale
    qs_ref[0] = jnp.where(lane < HEAD_DIM, q, 0.0).astype(BF16)
    qs_ref[1] = jnp.where(lane >= HEAD_DIM, q, 0.0).astype(BF16)
    acc_ref[...] = jnp.zeros_like(acc_ref)
    run_ref[...] = jnp.zeros_like(run_ref)
    bias = [bias_ref[2 * hp], bias_ref[2 * hp + 1]]

    def block(kb, masked):
        start = pl.multiple_of(kb * tk, tk)
        kblk = k_ref[pl.ds(start, tk), :].astype(BF16)
        vblk = v_ref[pl.ds(start, tk), :].astype(BF16)
        tri = tri_ref[...]
        if masked:
            row = lax.broadcasted_iota(jnp.int32, (tq, tk), 0)
            colm = lax.broadcasted_iota(jnp.int32, (tq, tk), 1)
            causal = colm < row
        for s in range(2):
            z = lax.dot_general(qs_ref[s], kblk, (((1,), (1,)), ((), ())),
                                preferred_element_type=F32) + bias[s]
            log_keep = -_softplus(z)
            if masked:
                log_keep = jnp.where(causal, log_keep, 0.0)
            hi, lo = _split_bf16(log_keep)
            sums = (jnp.dot(hi, tri, preferred_element_type=F32)
                    + jnp.dot(lo, tri, preferred_element_type=F32))
            later = sums[:, :tk] + run_ref[s]
            att = jnp.exp(z + log_keep + later)
            if masked:
                att = jnp.where(causal, att, 0.0)
            acc_ref[s] += jnp.dot(att.astype(BF16), vblk, preferred_element_type=F32)
            run_ref[s] += sums[:, tk:]

    block(qi, True)

    def body(i, carry):
        block(qi - 1 - i, False)
        return carry

    lax.fori_loop(0, qi, body, 0)
    o_ref[...] = jnp.where(lane < HEAD_DIM, acc_ref[0], acc_ref[1])


def _sb_tri(tk):
    j = np.arange(tk)[:, None]
    s = np.arange(tk)[None, :]
    return jnp.asarray(np.concatenate([(j > s), np.ones((tk, tk), bool)], axis=1), dtype=BF16)


def _sb_seq(p, bias, n_heads, tq):
    t = p.shape[0]
    assert t % tq == 0 and n_heads % 2 == 0
    pairs = n_heads // 2
    scale = 1.0 / math.sqrt(HEAD_DIM)
    return pl.pallas_call(
        functools.partial(_sb_seq_kernel, scale),
        grid=(pairs, t // tq),
        in_specs=[pl.BlockSpec(memory_space=pltpu.SMEM),
                  pl.BlockSpec((tq, LANES), lambda h, i: (i, h)),
                  pl.BlockSpec((t, LANES), lambda h, i: (0, pairs + h)),
                  pl.BlockSpec((t, LANES), lambda h, i: (0, 2 * pairs + h)),
                  pl.BlockSpec((tq, 2 * tq), lambda h, i: (0, 0))],
        out_specs=pl.BlockSpec((tq, LANES), lambda h, i: (i, h)),
        out_shape=jax.ShapeDtypeStruct((t, n_heads * HEAD_DIM), F32),
        scratch_shapes=[pltpu.VMEM((2, tq, LANES), BF16), pltpu.VMEM((2, tq, LANES), F32),
                        pltpu.VMEM((2, tq, tq), F32)],
        compiler_params=_params("parallel", "arbitrary"),
        name="sb_seq",
    )(bias, p, p, p, _sb_tri(tq))


def _sb_tok_kernel(n_pp, scale, pt_ref, bias_ref, q_ref, *refs):
    k_refs = refs[:n_pp]
    v_refs = refs[n_pp:2 * n_pp]
    sel_ref, exp_ref, tri_ref, o_ref, acc_ref, run_ref = refs[2 * n_pp:]
    j = pl.program_id(1)

    @pl.when(j == 0)
    def _():
        acc_ref[...] = jnp.zeros_like(acc_ref)
        run_ref[...] = jnp.zeros_like(run_ref)

    q = q_ref[...] * scale
    sel = sel_ref[...]
    expand = exp_ref[...]
    tri = tri_ref[...]
    bias = bias_ref[...]
    for i in range(n_pp):
        kpage = k_refs[i][...]
        z = jnp.dot((kpage * q).astype(BF16), sel, preferred_element_type=F32) + bias
        log_keep = -_softplus(z)
        hi, lo = _split_bf16(log_keep)
        later = (jnp.dot(tri, hi, preferred_element_type=F32)
                 + jnp.dot(tri, lo, preferred_element_type=F32)) + run_ref[...]
        att = jnp.exp(z + log_keep + later)
        att_wide = jnp.dot(att.astype(BF16), expand, preferred_element_type=F32)
        acc_ref[...] += att_wide * v_refs[i][...]
        run_ref[...] += jnp.sum(log_keep, axis=0, keepdims=True)

    @pl.when(j == pl.num_programs(1) - 1)
    def _():
        o_ref[...] = jnp.sum(acc_ref[...], axis=0, keepdims=True)


def _sb_tok(q, bias, cache_k, cache_v, page_table, n_pp):
    bsz, _, width = q.shape
    _, page, _ = cache_k.shape
    n_pages = page_table.shape[1]
    n_heads = width // HEAD_DIM
    assert n_pages % n_pp == 0 and n_heads <= LANES
    steps = n_pages // n_pp
    scale = 1.0 / math.sqrt(HEAD_DIM)
    head_of_lane = np.arange(width) // HEAD_DIM
    sel = jnp.asarray(head_of_lane[:, None] == np.arange(LANES)[None, :], dtype=BF16)
    expand = jnp.asarray(np.arange(LANES)[:, None] == head_of_lane[None, :], dtype=BF16)
    tri = jnp.asarray(np.arange(page)[None, :] > np.arange(page)[:, None], dtype=BF16)
    bias_row = jnp.zeros((1, LANES), F32).at[0, :n_heads].set(bias)

    def page_spec(i):
        return pl.BlockSpec((None, page, width),
                            lambda b, j, pt, i=i: (pt[b, n_pages - 1 - (j * n_pp + i)], 0, 0))

    def const(shape):
        return pl.BlockSpec(shape, lambda b, j, pt: (0,) * len(shape))

    grid_spec = pltpu.PrefetchScalarGridSpec(
        num_scalar_prefetch=1,
        grid=(bsz, steps),
        in_specs=[const((1, LANES)),
                  pl.BlockSpec((None, 1, width), lambda b, j, pt: (b, 0, 0))]
                 + [page_spec(i) for i in range(n_pp)] * 2
                 + [const((width, LANES)), const((LANES, width)), const((page, page))],
        out_specs=pl.BlockSpec((None, 1, width), lambda b, j, pt: (b, 0, 0)),
        scratch_shapes=[pltpu.VMEM((page, width), F32), pltpu.VMEM((1, LANES), F32)],
    )
    return pl.pallas_call(
        functools.partial(_sb_tok_kernel, n_pp, scale),
        grid_spec=grid_spec,
        out_shape=jax.ShapeDtypeStruct((bsz, 1, width), F32),
        compiler_params=_params("parallel", "arbitrary"),
        name="sb_tok",
    )(page_table, bias_row, q, *([cache_k] * n_pp), *([cache_v] * n_pp), sel, expand, tri)


def _mix_kernel(alpha, y_ref, bon_ref, gate_ref, sb_ref, x_ref, wo_ref, lg_ref, lb_ref,
                g1_ref, b1_ref, ones_ref, h_ref):
    ones = ones_ref[...]
    rw_width = y_ref.shape[1]
    y = y_ref[...]
    inv_n = 1.0 / HEAD_DIM
    mean = _head_sum(y, ones) * inv_n
    yc = y - mean
    var = _head_sum(yc * yc, ones) * inv_n
    yn = yc * lax.rsqrt(var + GN_EPS) * lg_ref[...] + lb_ref[...]
    rw = (yn + bon_ref[...]) * gate_ref[...]
    mix = (jnp.dot(rw.astype(BF16), wo_ref[0:rw_width, :], preferred_element_type=F32)
           + jnp.dot(sb_ref[...].astype(BF16), wo_ref[rw_width:, :], preferred_element_type=F32))
    h_ref[...] = _layer_norm(alpha * x_ref[...] + mix, g1_ref[...], b1_ref[...])


def _mix(y, bonus, gate, sb, x, wo, lnx_g, lnx_b, ln1_g, ln1_b, alpha, tm):
    m, d = x.shape
    rw_width, sb_width = y.shape[1], sb.shape[1]
    assert m % tm == 0

    def tile(width):
        return pl.BlockSpec((tm, width), lambda i: (i, 0))

    def whole(shape):
        return pl.BlockSpec(shape, lambda i: (0,) * len(shape))

    return pl.pallas_call(
        functools.partial(_mix_kernel, alpha),
        grid=(m // tm,),
        in_specs=[tile(rw_width), tile(rw_width), tile(rw_width), tile(sb_width), tile(d),
                  whole(wo.shape), whole((1, rw_width)), whole((1, rw_width)),
                  whole((1, d)), whole((1, d)), whole((SEG, SEG))],
        out_specs=tile(d),
        out_shape=jax.ShapeDtypeStruct((m, d), F32),
        compiler_params=_params("parallel"),
        name="mix_ln",
    )(y, bonus, gate, sb, x, wo, lnx_g, lnx_b, ln1_g, ln1_b, _head_ones())


def _top_values(x, count):
    rows = lax.broadcasted_iota(jnp.int32, x.shape, 0)
    vals = []
    for _ in range(count):
        m = jnp.max(x, axis=0, keepdims=True)
        vals.append(m)
        first = jnp.min(jnp.where(x == m, rows, x.shape[0]), axis=0, keepdims=True)
        x = jnp.where(rows == first, NEG_INF, x)
    return vals


def _rank_pairs():
    return [(m, n) for m in range(TOPK) for n in range(TOPK) if (m + 1) * (n + 1) <= TOPK]


def _peer_stats_kernel(q_ref, sk_ref, s1_ref, e1_ref, s2_ref, e2_ref, tau_ref, cand_ref):
    n_heads = s1_ref.shape[0]
    dk = sk_ref.shape[2]
    pairs = _rank_pairs()
    cand_ref[...] = jnp.full(cand_ref.shape, NEG_INF, F32)
    for h in range(n_heads):
        scores = []
        for c in range(2):
            qc = q_ref[:, (2 * h + c) * dk:(2 * h + c + 1) * dk].astype(BF16)
            scores.append(lax.dot_general(sk_ref[2 * h + c], qc, (((1,), (1,)), ((), ())),
                                          preferred_element_type=F32))
        top1 = _top_values(scores[0], TOPK)
        top2 = _top_values(scores[1], TOPK)
        for idx, (m, n) in enumerate(pairs):
            cand_ref[idx:idx + 1, :] = top1[m] + top2[n]
        best = _top_values(cand_ref[...], TOPK)
        z = sum(jnp.exp(b - best[0]) for b in best)
        s1_ref[h] = scores[0]
        s2_ref[h] = scores[1]
        e1_ref[h] = jnp.exp(scores[0] - top1[0]) / z
        e2_ref[h] = jnp.exp(scores[1] - top2[0])
        tau_ref[h:h + 1, :] = best[TOPK - 1]


def _peer_stats(q, subkeys, tb):
    m = q.shape[0]
    n_hc, n_keys, dk = subkeys.shape
    n_heads = n_hc // 2
    assert m % tb == 0
    stat = pl.BlockSpec((n_heads, n_keys, tb), lambda i: (0, 0, i))
    stat_shape = jax.ShapeDtypeStruct((n_heads, n_keys, m), F32)
    return pl.pallas_call(
        _peer_stats_kernel,
        grid=(m // tb,),
        in_specs=[pl.BlockSpec((tb, n_hc * dk), lambda i: (i, 0)),
                  pl.BlockSpec((n_hc, n_keys, dk), lambda i: (0, 0, 0))],
        out_specs=[stat, stat, stat, stat, pl.BlockSpec((n_heads, tb), lambda i: (0, i))],
        out_shape=[stat_shape] * 4 + [jax.ShapeDtypeStruct((n_heads, m), F32)],
        scratch_shapes=[pltpu.VMEM((-(-len(_rank_pairs()) // 8) * 8, tb), F32)],
        compiler_params=_params("parallel"),
        name="peer_stats",
    )(q, subkeys)


def _peer_mix_kernel(alpha, n_i, h_ref, u_ref, vt_ref, s1_ref, e1_ref, s2_ref, e2_ref, tau_ref,
                     g2_ref, b2_ref, y_ref, hb_ref, coef_ref, acc_ref):
    j = pl.program_id(1)
    n_heads, n_keys, tm = s2_ref.shape

    @pl.when(j == 0)
    def _():
        hb_ref[...] = h_ref[...].astype(BF16)
        acc_ref[...] = jnp.zeros_like(acc_ref)

    hb = hb_ref[...]
    for il in range(n_i):
        act = lax.dot_general(u_ref[il * n_keys:(il + 1) * n_keys, :], hb, (((1,), (1,)), ((), ())),
                              preferred_element_type=F32)
        gate = jnp.zeros((n_keys, tm), F32)
        for h in range(n_heads):
            s1 = s1_ref[h, il:il + 1, :]
            e1 = e1_ref[h, il:il + 1, :]
            tau = tau_ref[h:h + 1, :]
            gate = gate + jnp.where(s1 + s2_ref[h] >= tau, e2_ref[h], 0.0) * e1
        gelu = 0.5 * act * (1.0 + lax.erf(act * math.sqrt(0.5)))
        coef_ref[il * n_keys:(il + 1) * n_keys, :] = (gate * gelu).astype(BF16)
    acc_ref[...] += jnp.dot(vt_ref[...], coef_ref[...], preferred_element_type=F32)

    @pl.when(j == pl.num_programs(1) - 1)
    def _():
        y_ref[...] = _layer_norm(alpha * h_ref[...] + acc_ref[...].T, g2_ref[...], b2_ref[...])


def _peer_mix(h, u, vt, s1, e1, s2, e2, tau, ln2_g, ln2_b, alpha, tm, n_i):
    m, d = h.shape
    n_heads, n_keys, _ = s2.shape
    assert m % tm == 0 and n_keys % n_i == 0 and n_i % 8 == 0
    tn = n_i * n_keys
    row_stat = pl.BlockSpec((n_heads, n_i, tm), lambda i, j: (0, j, i))
    once = pl.Buffered(1)
    full_stat = pl.BlockSpec((n_heads, n_keys, tm), lambda i, j: (0, 0, i), pipeline_mode=once)
    return pl.pallas_call(
        functools.partial(_peer_mix_kernel, alpha, n_i),
        grid=(m // tm, n_keys // n_i),
        in_specs=[pl.BlockSpec((tm, d), lambda i, j: (i, 0), pipeline_mode=once),
                  pl.BlockSpec((tn, d), lambda i, j: (j, 0)),
                  pl.BlockSpec((d, tn), lambda i, j: (0, j)),
                  row_stat, row_stat, full_stat, full_stat,
                  pl.BlockSpec((n_heads, tm), lambda i, j: (0, i)),
                  pl.BlockSpec((1, d), lambda i, j: (0, 0)),
                  pl.BlockSpec((1, d), lambda i, j: (0, 0))],
        out_specs=pl.BlockSpec((tm, d), lambda i, j: (i, 0)),
        out_shape=jax.ShapeDtypeStruct((m, d), F32),
        scratch_shapes=[pltpu.VMEM((tm, d), BF16), pltpu.VMEM((tn, tm), BF16), pltpu.VMEM((d, tm), F32)],
        compiler_params=_params("parallel", "arbitrary"),
        name="peer_mix",
    )(h, u, vt, s1, e1, s2, e2, tau, ln2_g, ln2_b)


def _pick_tile(m, target):
    t = min(m, target)
    while m % t:
        t //= 2
    return t


def _layer_weights(w_in, mu_shift, w0, w_decay_up, a0, w_aaa_up, w_gate_up, k_k, k_a, r_k,
                   lnx_g, lnx_b, w_out, ln1_g, ln1_b, peer_wq, peer_subkeys, peer_u, peer_v,
                   ln2_g, ln2_b, sb_width):
    rw_width = w0.shape[0]
    d_lora, a_lora, g_lora = w_decay_up.shape[0], w_aaa_up.shape[0], w_gate_up.shape[0]
    in_cols = w_in.shape[1]
    rw_cols = in_cols - 3 * sb_width
    tail = rw_cols - 3 * rw_width
    assert tail == d_lora + a_lora + g_lora
    tail_w = -(-tail // LANES) * LANES
    cols_pad = 3 * sb_width + 3 * rw_width + tail_w
    assert sb_width % rw_width == 0 and (3 * sb_width + 3 * rw_width) % tail_w == 0
    row = lambda x: x.reshape(1, -1).astype(F32)
    mu = jnp.pad(mu_shift, (0, tail_w - tail))

    def lora(w, start):
        return jnp.zeros((tail_w, rw_width), F32).at[start:start + w.shape[0]].set(w).astype(BF16)

    n_ph, _, n_keys, dk = peer_subkeys.shape
    return dict(
        w_in=jnp.pad(w_in, ((0, 0), (0, cols_pad - in_cols))).astype(BF16),
        rw_prm=[row(mu[:rw_width]), row(mu[rw_width:2 * rw_width]), row(mu[2 * rw_width:3 * rw_width]),
                row(mu[3 * rw_width:]), row(w0), row(a0), row(k_k), row(k_a), row(r_k),
                lora(w_decay_up, 0), lora(w_aaa_up, d_lora), lora(w_gate_up, d_lora + a_lora),
                _head_ones()],
        lnx_g=row(lnx_g), lnx_b=row(lnx_b), w_out=w_out.astype(BF16),
        ln1_g=row(ln1_g), ln1_b=row(ln1_b), ln2_g=row(ln2_g), ln2_b=row(ln2_b),
        peer_wq=peer_wq.astype(BF16),
        subkeys=peer_subkeys.reshape(n_ph * 2, n_keys, dk).astype(BF16),
        peer_u=peer_u.astype(BF16), peer_vt=peer_v.astype(BF16).T,
        rw_width=rw_width, rw_cols=rw_cols, tail_w=tail_w, cols_pad=cols_pad,
    )


def _in_proj(x, lw):
    m = x.shape[0]
    n = lw["cols_pad"]
    tn = n // 3 if n % (3 * LANES) == 0 else n
    return _matmul(x, lw["w_in"], _pick_tile(m, 512), tn, "in_proj")


def _peer(h, lw, alpha):
    m = h.shape[0]
    m_pad = -(-m // LANES) * LANES
    hp = jnp.pad(h, ((0, m_pad - m), (0, 0))) if m_pad != m else h
    q = _matmul(hp, lw["peer_wq"], _pick_tile(m_pad, 512), lw["peer_wq"].shape[1], "peer_q")
    s1, e1, s2, e2, tau = _peer_stats(q, lw["subkeys"], _pick_tile(m_pad, 256))
    y = _peer_mix(hp, lw["peer_u"], lw["peer_vt"], s1, e1, s2, e2, tau, lw["ln2_g"], lw["ln2_b"],
                  alpha, _pick_tile(m_pad, 512), 8)
    return y[:m]


def _wkv_to_lanes(s):
    b, h, v, k = s.shape
    return s.transpose(0, 2, 1, 3).reshape(b, v, h * k)


def _wkv_from_lanes(s, n_heads):
    b, v, hk = s.shape
    return s.reshape(b, v, n_heads, hk // n_heads).transpose(0, 2, 1, 3)


def kernel(x_prompt, x_sample, cache_sb_k, cache_sb_v, state_wkv, state_shift, page_table, w_in, sb_bias, mu_shift, w0, w_decay_up, a0, w_aaa_up, w_gate_up, k_k, k_a, r_k, lnx_g, lnx_b, w_out, ln1_g, ln1_b, peer_wq, peer_subkeys, peer_u, peer_v, ln2_g, ln2_b):
    depth = w_in.shape[0]
    bsz, seq, d_model = x_prompt.shape
    dec_b, dec_seq, _ = x_sample.shape
    assert dec_seq == 1, "one new token per sampled sequence"
    _, n_pool, page, sb_heads, head_dim = cache_sb_k.shape
    rw_heads = state_wkv.shape[2]
    assert head_dim == HEAD_DIM and state_wkv.shape[3] == HEAD_DIM
    sb_width = sb_heads * HEAD_DIM
    alpha = (2 * depth) ** 0.25

    h_p = [x_prompt[b] for b in range(bsz)]
    h_s = x_sample[:, 0]
    outs = {name: [] for name in ("kp", "vp", "wp", "sp", "ks", "vs", "ws", "ss")}
    for l in range(depth):
        lw = _layer_weights(w_in[l], mu_shift[l], w0[l], w_decay_up[l], a0[l], w_aaa_up[l], w_gate_up[l],
                            k_k[l], k_a[l], r_k[l].reshape(-1), lnx_g[l], lnx_b[l], w_out[l], ln1_g[l],
                            ln1_b[l], peer_wq[l], peer_subkeys[l], peer_u[l], peer_v[l], ln2_g[l],
                            ln2_b[l], sb_width)
        rw_width, rw_cols, tail_w = lw["rw_width"], lw["rw_cols"], lw["tail_w"]
        rw0 = 3 * sb_width

        kp, vp, wp, sp = [], [], [], []
        for b in range(bsz):
            x = h_p[b]
            p = _in_proj(x, lw)
            kp.append(p[:, sb_width:2 * sb_width].reshape(seq, sb_heads, HEAD_DIM))
            vp.append(p[:, 2 * sb_width:3 * sb_width].reshape(seq, sb_heads, HEAD_DIM))
            sp.append(p[seq - 1:seq, rw0:rw0 + rw_cols])
            zeros = [jnp.zeros((1, rw_width), F32)] * 3 + [jnp.zeros((1, tail_w), F32)]
            r, w, k, v, a, bb, gate, bonus = _rw_prep(p, None, zeros, lw["rw_prm"], _pick_tile(seq, 256),
                                                      rw_width, rw0, tail_w)
            y, s_fin = _rw_scan([u[None] for u in (r, w, k, v, a, bb)],
                                jnp.zeros((1, HEAD_DIM, rw_width), F32), _pick_tile(seq, 128))
            wp.append(_wkv_from_lanes(s_fin, rw_heads)[0])
            sb = _sb_seq(p, sb_bias[l], sb_heads, _pick_tile(seq, 256))
            h = _mix(y[0], bonus, gate, sb, x, lw["w_out"], lw["lnx_g"], lw["lnx_b"], lw["ln1_g"],
                     lw["ln1_b"], alpha, _pick_tile(seq, 256))
            h_p[b] = _peer(h, lw, alpha)
        outs["kp"].append(jnp.stack(kp)); outs["vp"].append(jnp.stack(vp))
        outs["wp"].append(jnp.stack(wp)); outs["sp"].append(jnp.stack(sp))

        p = _in_proj(h_s, lw)
        outs["ks"].append(p[:, sb_width:2 * sb_width].reshape(dec_b, 1, sb_heads, HEAD_DIM))
        outs["vs"].append(p[:, 2 * sb_width:3 * sb_width].reshape(dec_b, 1, sb_heads, HEAD_DIM))
        outs["ss"].append(p[:, None, rw0:rw0 + rw_cols])
        shift = state_shift[l][:, 0]
        prev = [shift[:, :rw_width], shift[:, rw_width:2 * rw_width], shift[:, 2 * rw_width:3 * rw_width],
                jnp.pad(shift[:, 3 * rw_width:], ((0, 0), (0, tail_w - (rw_cols - 3 * rw_width))))]
        r, w, k, v, a, bb, gate, bonus = _rw_prep(p, prev, None, lw["rw_prm"], _pick_tile(dec_b, 256),
                                                  rw_width, rw0, tail_w)
        y, s_fin = _rw_scan([u[:, None] for u in (r, w, k, v, a, bb)], _wkv_to_lanes(state_wkv[l]), 1)
        outs["ws"].append(_wkv_from_lanes(s_fin, rw_heads))
        sb = _sb_tok(p[:, None, :sb_width], sb_bias[l],
                     cache_sb_k[l].reshape(n_pool, page, sb_width), cache_sb_v[l].reshape(n_pool, page, sb_width),
                     page_table, 2 if page_table.shape[1] % 2 == 0 else 1)
        h = _mix(y[:, 0], bonus, gate, sb[:, 0], h_s, lw["w_out"], lw["lnx_g"], lw["lnx_b"], lw["ln1_g"],
                 lw["ln1_b"], alpha, _pick_tile(dec_b, 256))
        h_s = _peer(h, lw, alpha)

    stack = lambda name: jnp.stack(outs[name])
    return (jnp.stack(h_p), h_s[:, None], stack("kp"), stack("vp"), stack("wp"), stack("sp"),
            stack("ks"), stack("vs"), stack("ws"), stack("ss"))
```

```python
import functools
import math

import numpy as np
import jax
import jax.numpy as jnp
from jax import lax
from jax.experimental import pallas as pl
from jax.experimental.pallas import tpu as pltpu

F32 = jnp.float32
BF16 = jnp.bfloat16

LANES = 128
HEAD_DIM = 64
SEG = 256
GN_EPS = 64e-5
LN_EPS = 1e-5
TOPK = 16
NEG_INF = float("-inf")
VMEM_LIMIT_BYTES = 56 * 1024 * 1024


def _params(*semantics):
    return pltpu.CompilerParams(dimension_semantics=semantics, vmem_limit_bytes=VMEM_LIMIT_BYTES)


def _head_ones():
    blocks = SEG // HEAD_DIM
    return jnp.asarray(np.kron(np.eye(blocks), np.ones((HEAD_DIM, HEAD_DIM))), dtype=BF16)


def _split_bf16(x):
    hi = x.astype(BF16)
    lo = (x - hi.astype(F32)).astype(BF16)
    return hi, lo


def _head_sum(x, ones):
    outs = []
    for g in range(x.shape[-1] // SEG):
        hi, lo = _split_bf16(x[:, g * SEG:(g + 1) * SEG])
        outs.append(jnp.dot(hi, ones, preferred_element_type=F32)
                    + jnp.dot(lo, ones, preferred_element_type=F32))
    return outs[0] if len(outs) == 1 else jnp.concatenate(outs, axis=-1)


def _softplus(x):
    return jnp.maximum(x, 0.0) + jnp.log1p(jnp.exp(-jnp.abs(x)))


def _softplus_pos(x):
    return jnp.maximum(x, 0.0) + jnp.log(1.0 + jnp.exp(-jnp.abs(x)))


def _layer_norm(x, g, b):
    mu = jnp.mean(x, axis=-1, keepdims=True)
    xc = x - mu
    var = jnp.mean(xc * xc, axis=-1, keepdims=True)
    return xc * lax.rsqrt(var + LN_EPS) * g + b


def _matmul_kernel(x_ref, w_ref, o_ref):
    o_ref[...] = jnp.dot(x_ref[...].astype(BF16), w_ref[...], preferred_element_type=F32)


def _matmul(x, w, tm, tn, name):
    m, k = x.shape
    n = w.shape[1]
    assert m % tm == 0 and n % tn == 0
    return pl.pallas_call(
        _matmul_kernel,
        grid=(n // tn, m // tm),
        in_specs=[pl.BlockSpec((tm, k), lambda j, i: (i, 0)),
                  pl.BlockSpec((k, tn), lambda j, i: (0, j))],
        out_specs=pl.BlockSpec((tm, tn), lambda j, i: (i, j)),
        out_shape=jax.ShapeDtypeStruct((m, n), F32),
        compiler_params=_params("parallel", "parallel"),
        name=name,
    )(x, w)


def _rw_prep_kernel(halo, *refs):
    if halo:
        (pr, pk, pv, pt, hr, hk, hv, ht, sr, sk, sv, st,
         mur, muk, muv, mut, w0, a0, kk_s, ka_s, rk_s, wd, wa, wg, ones_ref,
         r_o, w_o, k_o, v_o, a_o, b_o, g_o, bon_o) = refs
    else:
        (pr, pk, pv, pt, qr, qk, qv, qt,
         mur, muk, muv, mut, w0, a0, kk_s, ka_s, rk_s, wd, wa, wg, ones_ref,
         r_o, w_o, k_o, v_o, a_o, b_o, g_o, bon_o) = refs

    def shifted(x_ref, idx):
        x = x_ref[...]
        if not halo:
            return x, (qr, qk, qv, qt)[idx][...]
        h_ref = (hr, hk, hv, ht)[idx]
        s_ref = (sr, sk, sv, st)[idx]
        first = jnp.where(pl.program_id(0) == 0, s_ref[...], h_ref[7:8, :])
        row = lax.broadcasted_iota(jnp.int32, x.shape, 0)
        return x, jnp.where(row == 0, first, pltpu.roll(x, 1, axis=0))

    def lerp(x_ref, idx, mu_ref):
        x, prev = shifted(x_ref, idx)
        return x + (prev - x) * mu_ref[...]

    ones = ones_ref[...]
    xr = lerp(pr, 0, mur)
    xk = lerp(pk, 1, muk)
    xv = lerp(pv, 2, muv)
    xt = lerp(pt, 3, mut)

    dec_in = jnp.dot(jnp.tanh(xt).astype(BF16), wd[...], preferred_element_type=F32)
    w = -_softplus(-(w0[...] + dec_in)) - 0.5
    decay = jnp.exp(-jnp.exp(w))
    iclr = jax.nn.sigmoid(a0[...] + jnp.dot(xt.astype(BF16), wa[...], preferred_element_type=F32))
    gate = jnp.dot(jax.nn.sigmoid(xt).astype(BF16), wg[...], preferred_element_type=F32)

    kk = xk * kk_s[...]
    norm = jnp.sqrt(_head_sum(kk * kk, ones))
    kk = kk / jnp.maximum(norm, 1e-12)
    k2 = xk * (1.0 + (iclr - 1.0) * ka_s[...])
    bonus = _head_sum(xr * k2 * rk_s[...], ones) * xv

    r_o[...] = xr
    w_o[...] = decay
    k_o[...] = k2
    v_o[...] = xv
    a_o[...] = -kk
    b_o[...] = kk * iclr
    g_o[...] = gate
    bon_o[...] = bonus


def _rw_prep(p, prev, shift_rows, prm, tm, rw_width, sb_cols, tail_w):
    m = p.shape[0]
    assert m % tm == 0
    cb = sb_cols // rw_width
    tb = (sb_cols + 3 * rw_width) // tail_w
    assert cb * rw_width == sb_cols and tb * tail_w == sb_cols + 3 * rw_width
    halo = prev is None

    def col(width, c):
        return pl.BlockSpec((tm, width), lambda i, c=c: (i, c))

    def halo_col(width, c):
        return pl.BlockSpec((8, width), lambda i, c=c: (jnp.maximum(i * (tm // 8) - 1, 0), c))

    def row(width):
        return pl.BlockSpec((1, width), lambda i: (0, 0))

    def whole(shape):
        return pl.BlockSpec(shape, lambda i: (0,) * len(shape))

    main_specs = [col(rw_width, cb), col(rw_width, cb + 1), col(rw_width, cb + 2), col(tail_w, tb)]
    if halo:
        assert tm % 8 == 0
        extra_specs = [halo_col(rw_width, cb), halo_col(rw_width, cb + 1), halo_col(rw_width, cb + 2),
                       halo_col(tail_w, tb), row(rw_width), row(rw_width), row(rw_width), row(tail_w)]
        extra_args = [p, p, p, p] + list(shift_rows)
    else:
        extra_specs = [pl.BlockSpec((tm, rw_width), lambda i: (i, 0))] * 3 + [pl.BlockSpec((tm, tail_w), lambda i: (i, 0))]
        extra_args = list(prev)
    prm_specs = [row(rw_width), row(rw_width), row(rw_width), row(tail_w),
                 row(rw_width), row(rw_width), row(rw_width), row(rw_width), row(rw_width),
                 whole((tail_w, rw_width)), whole((tail_w, rw_width)), whole((tail_w, rw_width)),
                 whole((SEG, SEG))]
    out_spec = pl.BlockSpec((tm, rw_width), lambda i: (i, 0))
    out_shape = jax.ShapeDtypeStruct((m, rw_width), F32)
    return pl.pallas_call(
        functools.partial(_rw_prep_kernel, halo),
        grid=(m // tm,),
        in_specs=main_specs + extra_specs + prm_specs,
        out_specs=[out_spec] * 8,
        out_shape=[out_shape] * 8,
        compiler_params=_params("parallel"),
        name="rw_prep_seq" if halo else "rw_prep_tok",
    )(p, p, p, p, *extra_args, *prm)


def _rw_scan_kernel(r_ref, w_ref, k_ref, v_ref, a_ref, b_ref, s0_ref, ones_ref,
                    y_ref, sT_ref, s_ref):
    tc = r_ref.shape[0]
    width = r_ref.shape[1]
    groups = width // SEG

    @pl.when(pl.program_id(1) == 0)
    def _():
        for g in range(groups):
            s_ref[g] = s0_ref[:, g * SEG:(g + 1) * SEG]

    ones = ones_ref[...]
    lane = lax.broadcasted_iota(jnp.int32, (HEAD_DIM, SEG), 1)
    sub = lax.broadcasted_iota(jnp.int32, (HEAD_DIM, SEG), 0)
    diag = (lane % HEAD_DIM) == sub

    def step(t, carry):
        r = r_ref[pl.ds(t, 1), :]
        w = w_ref[pl.ds(t, 1), :]
        k = k_ref[pl.ds(t, 1), :]
        v = v_ref[pl.ds(t, 1), :]
        a = a_ref[pl.ds(t, 1), :]
        b = b_ref[pl.ds(t, 1), :]
        v_hi = v.astype(BF16).astype(F32)
        v_lo = v - v_hi
        sls = [slice(g * SEG, (g + 1) * SEG) for g in range(groups)]
        n = groups * HEAD_DIM
        lhs = jnp.concatenate(
            [jnp.where(diag, v_hi[:, sl], 0.0).astype(BF16) for sl in sls]
            + [jnp.where(diag, v_lo[:, sl], 0.0).astype(BF16) for sl in sls]
            + [(s_ref[g] * a[:, sls[g]]).astype(BF16) for g in range(groups)], axis=0)
        res = jnp.dot(lhs, ones, preferred_element_type=F32)
        outs = []
        for g in range(groups):
            sl = sls[g]
            rows = slice(g * HEAD_DIM, (g + 1) * HEAD_DIM)
            vb = res[0:n][rows] + res[n:2 * n][rows]
            sa = res[2 * n:3 * n][rows]
            s_new = s_ref[g] * w[:, sl] + sa * b[:, sl] + vb * k[:, sl]
            s_ref[g] = s_new
            outs.append((s_new * r[:, sl]).astype(BF16))
        yb = jnp.dot(jnp.concatenate(outs, axis=0), ones, preferred_element_type=F32)
        ys = [jnp.sum(jnp.where(diag, yb[g * HEAD_DIM:(g + 1) * HEAD_DIM], 0.0), axis=0, keepdims=True)
              for g in range(groups)]
        y_ref[pl.ds(t, 1), :] = jnp.concatenate(ys, axis=-1)
        return carry

    lax.fori_loop(0, tc, step, 0, unroll=2 if tc % 2 == 0 else 1)

    @pl.when(pl.program_id(1) == pl.num_programs(1) - 1)
    def _():
        for g in range(groups):
            sT_ref[:, g * SEG:(g + 1) * SEG] = s_ref[g]


def _rw_scan(seqs, s0, tc):
    bsz, t, width = seqs[0].shape
    assert t % tc == 0 and width % SEG == 0
    seq_spec = pl.BlockSpec((None, tc, width), lambda b, c: (b, c, 0))
    st_spec = pl.BlockSpec((None, HEAD_DIM, width), lambda b, c: (b, 0, 0))
    return pl.pallas_call(
        _rw_scan_kernel,
        grid=(bsz, t // tc),
        in_specs=[seq_spec] * 6 + [st_spec, pl.BlockSpec((SEG, SEG), lambda b, c: (0, 0))],
        out_specs=[seq_spec, st_spec],
        out_shape=[jax.ShapeDtypeStruct((bsz, t, width), F32),
                   jax.ShapeDtypeStruct((bsz, HEAD_DIM, width), F32)],
        scratch_shapes=[pltpu.VMEM((width // SEG, HEAD_DIM, SEG), F32)],
        compiler_params=_params("parallel", "arbitrary"),
        name="rw_scan",
    )(*seqs, s0, _head_ones())


def _sb_seq_kernel(scale, tk, bias_ref, q_ref, k_ref, v_ref, tri_ref, o_ref, qs_ref, acc_ref, run_ref):
    tq = q_ref.shape[0]
    per_q = tq // tk
    hp = pl.program_id(0)
    qi = pl.program_id(1)
    lane = lax.broadcasted_iota(jnp.int32, (tq, LANES), 1)
    q = q_ref[...] * scale
    qs_ref[0] = jnp.where(lane < HEAD_DIM, q, 0.0).astype(BF16)
    qs_ref[1] = jnp.where(lane >= HEAD_DIM, q, 0.0).astype(BF16)
    acc_ref[...] = jnp.zeros_like(acc_ref)
    run_ref[...] = jnp.zeros_like(run_ref)
    bias = [bias_ref[2 * hp], bias_ref[2 * hp + 1]]

    def block(kb, masked):
        start = pl.multiple_of(kb * tk, tk)
        kblk = k_ref[pl.ds(start, tk), :].astype(BF16)
        vblk = v_ref[pl.ds(start, tk), :].astype(BF16)
        tri = tri_ref[...]
        if masked:
            row = lax.broadcasted_iota(jnp.int32, (tq, tk), 0) + qi * tq
            colm = lax.broadcasted_iota(jnp.int32, (tq, tk), 1) + start
            causal = colm < row
        for s in range(2):
            z = lax.dot_general(qs_ref[s], kblk, (((1,), (1,)), ((), ())),
                                preferred_element_type=F32) + bias[s]
            drop = _softplus_pos(z)
            if masked:
                drop = jnp.where(causal, drop, 0.0)
            hi, lo = _split_bf16(drop)
            later = jnp.dot(jnp.concatenate([hi, lo], axis=1), tri, preferred_element_type=F32)
            att = jnp.exp(z - drop + later)
            if masked:
                att = jnp.where(causal, att, 0.0)
            pv = jnp.dot(att.astype(BF16), vblk, preferred_element_type=F32)
            acc_ref[s] += jnp.exp(run_ref[s]) * pv
            run_ref[s] += jnp.broadcast_to(later[:, 0:1] - drop[:, 0:1], (tq, LANES))

    for d in range(per_q):
        block(qi * per_q + per_q - 1 - d, True)

    def body(i, carry):
        for d in range(per_q):
            block((qi - 1 - i) * per_q + per_q - 1 - d, False)
        return carry

    lax.fori_loop(0, qi, body, 0)
    o_ref[...] = jnp.where(lane < HEAD_DIM, acc_ref[0], acc_ref[1])


def _sb_tri(tk):
    j = np.arange(tk)[:, None]
    s = np.arange(tk)[None, :]
    half = -(j > s).astype(np.float32)
    return jnp.asarray(np.concatenate([half, half], axis=0), dtype=BF16)


def _sb_seq(p, bias, n_heads, tq, tk):
    t = p.shape[0]
    assert t % tq == 0 and tq % tk == 0 and n_heads % 2 == 0
    pairs = n_heads // 2
    scale = 1.0 / math.sqrt(HEAD_DIM)
    return pl.pallas_call(
        functools.partial(_sb_seq_kernel, scale, tk),
        grid=(pairs, t // tq),
        in_specs=[pl.BlockSpec(memory_space=pltpu.SMEM),
                  pl.BlockSpec((tq, LANES), lambda h, i: (i, h)),
                  pl.BlockSpec((t, LANES), lambda h, i: (0, pairs + h)),
                  pl.BlockSpec((t, LANES), lambda h, i: (0, 2 * pairs + h)),
                  pl.BlockSpec((2 * tk, tk), lambda h, i: (0, 0))],
        out_specs=pl.BlockSpec((tq, LANES), lambda h, i: (i, h)),
        out_shape=jax.ShapeDtypeStruct((t, n_heads * HEAD_DIM), F32),
        scratch_shapes=[pltpu.VMEM((2, tq, LANES), BF16), pltpu.VMEM((2, tq, LANES), F32),
                        pltpu.VMEM((2, tq, LANES), F32)],
        compiler_params=_params("parallel", "arbitrary"),
        name="sb_seq",
    )(bias, p, p, p, _sb_tri(tk))


def _sb_tok_kernel(n_pp, scale, pt_ref, bias_ref, q_ref, *refs):
    k_refs = refs[:n_pp]
    v_refs = refs[n_pp:2 * n_pp]
    tri_ref, o_ref, acc_ref, run_ref = refs[2 * n_pp:]
    j = pl.program_id(1)
    width, page = acc_ref.shape
    n_heads = width // HEAD_DIM

    @pl.when(j == 0)
    def _():
        acc_ref[...] = jnp.zeros_like(acc_ref)
        run_ref[...] = jnp.zeros_like(run_ref)

    q = q_ref[...]
    tri = tri_ref[...]
    bias = bias_ref[...]
    for i in range(n_pp):
        prod = k_refs[i][...] * q
        z = jnp.sum(prod.reshape(n_heads, HEAD_DIM, page), axis=1) * scale + bias
        drop = _softplus_pos(z)
        hi, lo = _split_bf16(-drop)
        sums = jnp.dot(jnp.concatenate([hi, lo], axis=1), tri, preferred_element_type=F32)
        att = jnp.exp(z - drop + sums[:, :page] + run_ref[...])
        att_wide = jnp.broadcast_to(att[:, None, :], (n_heads, HEAD_DIM, page)).reshape(width, page)
        acc_ref[...] += att_wide * v_refs[i][...]
        run_ref[...] += sums[:, page:]

    @pl.when(j == pl.num_programs(1) - 1)
    def _():
        o_ref[...] = jnp.sum(acc_ref[...], axis=1, keepdims=True)


def _sb_tok(q, bias, cache_k, cache_v, page_table, n_pp):
    bsz, width = q.shape
    _, _, page = cache_k.shape
    n_pages = page_table.shape[1]
    n_heads = width // HEAD_DIM
    assert n_pages % n_pp == 0
    steps = n_pages // n_pp
    scale = 1.0 / math.sqrt(HEAD_DIM)
    jj = np.arange(page)[:, None]
    ss = np.arange(page)[None, :]
    half = np.concatenate([(jj > ss), np.ones((page, page), bool)], axis=1)
    tri = jnp.asarray(np.concatenate([half, half], axis=0), dtype=BF16)
    bias_rows = jnp.broadcast_to(bias.astype(F32)[:, None], (n_heads, page))
    q_cols = jnp.broadcast_to(q[:, :, None], (bsz, width, page))

    def page_spec(i):
        return pl.BlockSpec((None, width, page),
                            lambda b, j, pt, i=i: (pt[b, n_pages - 1 - (j * n_pp + i)], 0, 0))

    def const(shape):
        return pl.BlockSpec(shape, lambda b, j, pt: (0,) * len(shape))

    grid_spec = pltpu.PrefetchScalarGridSpec(
        num_scalar_prefetch=1,
        grid=(bsz, steps),
        in_specs=[const((n_heads, page)),
                  pl.BlockSpec((None, width, page), lambda b, j, pt: (b, 0, 0))]
                 + [page_spec(i) for i in range(n_pp)] * 2
                 + [const((2 * page, 2 * page))],
        out_specs=pl.BlockSpec((None, width, 1), lambda b, j, pt: (b, 0, 0)),
        scratch_shapes=[pltpu.VMEM((width, page), F32), pltpu.VMEM((n_heads, page), F32)],
    )
    return pl.pallas_call(
        functools.partial(_sb_tok_kernel, n_pp, scale),
        grid_spec=grid_spec,
        out_shape=jax.ShapeDtypeStruct((bsz, width, 1), F32),
        compiler_params=_params("parallel", "arbitrary"),
        name="sb_tok",
    )(page_table, bias_rows, q_cols, *([cache_k] * n_pp), *([cache_v] * n_pp), tri)


def _mix_kernel(alpha, y_ref, bon_ref, gate_ref, sb_ref, x_ref, wo_ref, lg_ref, lb_ref,
                g1_ref, b1_ref, ones_ref, h_ref):
    ones = ones_ref[...]
    rw_width = y_ref.shape[1]
    y = y_ref[...]
    inv_n = 1.0 / HEAD_DIM
    mean = _head_sum(y, ones) * inv_n
    yc = y - mean
    var = _head_sum(yc * yc, ones) * inv_n
    yn = yc * lax.rsqrt(var + GN_EPS) * lg_ref[...] + lb_ref[...]
    rw = (yn + bon_ref[...]) * gate_ref[...]
    mix = (jnp.dot(rw.astype(BF16), wo_ref[0:rw_width, :], preferred_element_type=F32)
           + jnp.dot(sb_ref[...].astype(BF16), wo_ref[rw_width:, :], preferred_element_type=F32))
    h_ref[...] = _layer_norm(alpha * x_ref[...] + mix, g1_ref[...], b1_ref[...])


def _mix(y, bonus, gate, sb, x, wo, lnx_g, lnx_b, ln1_g, ln1_b, alpha, tm):
    m, d = x.shape
    rw_width, sb_width = y.shape[1], sb.shape[1]
    assert m % tm == 0

    def tile(width):
        return pl.BlockSpec((tm, width), lambda i: (i, 0))

    def whole(shape):
        return pl.BlockSpec(shape, lambda i: (0,) * len(shape))

    return pl.pallas_call(
        functools.partial(_mix_kernel, alpha),
        grid=(m // tm,),
        in_specs=[tile(rw_width), tile(rw_width), tile(rw_width), tile(sb_width), tile(d),
                  whole(wo.shape), whole((1, rw_width)), whole((1, rw_width)),
                  whole((1, d)), whole((1, d)), whole((SEG, SEG))],
        out_specs=tile(d),
        out_shape=jax.ShapeDtypeStruct((m, d), F32),
        compiler_params=_params("parallel"),
        name="mix_ln",
    )(y, bonus, gate, sb, x, wo, lnx_g, lnx_b, ln1_g, ln1_b, _head_ones())


def _top_values(x, count):
    rows = lax.broadcasted_iota(jnp.int32, x.shape, 0)
    vals = []
    for _ in range(count):
        m = jnp.max(x, axis=0, keepdims=True)
        vals.append(m)
        first = jnp.min(jnp.where(x == m, rows, x.shape[0]), axis=0, keepdims=True)
        x = jnp.where(rows == first, NEG_INF, x)
    return vals


def _rank_pairs():
    return [(m, n) for m in range(TOPK) for n in range(TOPK) if (m + 1) * (n + 1) <= TOPK]


def _peer_stats_kernel(q_ref, sk_ref, s1_ref, e1_ref, s2_ref, e2_ref, tau_ref, cand_ref):
    n_heads = s1_ref.shape[0]
    dk = sk_ref.shape[2]
    pairs = _rank_pairs()
    cand_ref[...] = jnp.full(cand_ref.shape, NEG_INF, F32)
    for h in range(n_heads):
        scores = []
        for c in range(2):
            qc = q_ref[:, (2 * h + c) * dk:(2 * h + c + 1) * dk].astype(BF16)
            scores.append(lax.dot_general(sk_ref[2 * h + c], qc, (((1,), (1,)), ((), ())),
                                          preferred_element_type=F32))
        top1 = _top_values(scores[0], TOPK)
        top2 = _top_values(scores[1], TOPK)
        for idx, (m, n) in enumerate(pairs):
            cand_ref[idx:idx + 1, :] = top1[m] + top2[n]
        best = _top_values(cand_ref[...], TOPK)
        z = sum(jnp.exp(b - best[0]) for b in best)
        e1 = jnp.exp(scores[0] - top1[0]) / z
        e2 = jnp.exp(scores[1] - top2[0])
        for c in range(s1_ref.shape[1]):
            cols = slice(c * LANES, (c + 1) * LANES)
            s1_ref[h, c] = scores[0][:, cols]
            s2_ref[h, c] = scores[1][:, cols]
            e1_ref[h, c] = e1[:, cols]
            e2_ref[h, c] = e2[:, cols]
            tau_ref[c, h:h + 1, :] = best[TOPK - 1][:, cols]


def _peer_stats(q, subkeys, tb):
    m = q.shape[0]
    n_hc, n_keys, dk = subkeys.shape
    n_heads = n_hc // 2
    assert m % tb == 0 and tb % LANES == 0
    cb = tb // LANES
    stat = pl.BlockSpec((n_heads, cb, n_keys, LANES), lambda i: (0, i, 0, 0))
    stat_shape = jax.ShapeDtypeStruct((n_heads, m // LANES, n_keys, LANES), F32)
    return pl.pallas_call(
        _peer_stats_kernel,
        grid=(m // tb,),
        in_specs=[pl.BlockSpec((tb, n_hc * dk), lambda i: (i, 0)),
                  pl.BlockSpec((n_hc, n_keys, dk), lambda i: (0, 0, 0))],
        out_specs=[stat, stat, stat, stat, pl.BlockSpec((cb, n_heads, LANES), lambda i: (i, 0, 0))],
        out_shape=[stat_shape] * 4 + [jax.ShapeDtypeStruct((m // LANES, n_heads, LANES), F32)],
        scratch_shapes=[pltpu.VMEM((-(-len(_rank_pairs()) // 8) * 8, tb), F32)],
        compiler_params=_params("parallel"),
        name="peer_stats",
    )(q, subkeys)


def _peer_mix_kernel(alpha, n_i, h_ref, u_ref, vt_ref, s1_ref, e1_ref, s2_ref, e2_ref, tau_ref,
                     g2_ref, b2_ref, y_ref, hb_ref, act_ref, coef_ref, acc_ref):
    j = pl.program_id(1)
    n_heads, n_c, n_keys, _ = s2_ref.shape

    @pl.when(j == 0)
    def _():
        hb_ref[...] = h_ref[...].astype(BF16)
        acc_ref[...] = jnp.zeros_like(acc_ref)

    act = lax.dot_general(u_ref[...], hb_ref[...], (((1,), (1,)), ((), ())), preferred_element_type=F32)
    for c in range(n_c):
        act_ref[c] = act[:, c * LANES:(c + 1) * LANES]

    def tile(idx, carry):
        il = idx // n_c
        c = idx % n_c
        rows = pl.ds(pl.multiple_of(il * n_keys, n_keys), n_keys)
        gate = jnp.zeros((n_keys, LANES), F32)
        for h in range(n_heads):
            s1 = s1_ref[h, c, pl.ds(il, 1), :]
            e1 = e1_ref[h, c, pl.ds(il, 1), :]
            tau = tau_ref[c, h:h + 1, :]
            gate = gate + jnp.where(s1 + s2_ref[h, c] >= tau, e2_ref[h, c], 0.0) * e1
        a = act_ref[c, rows, :]
        gelu = 0.5 * a * (1.0 + lax.erf(a * math.sqrt(0.5)))
        coef_ref[c, rows, :] = (gate * gelu).astype(BF16)
        return carry

    lax.fori_loop(0, n_i * n_c, tile, 0)
    coef = jnp.concatenate([coef_ref[c] for c in range(n_c)], axis=1)
    acc_ref[...] += jnp.dot(vt_ref[...], coef, preferred_element_type=F32)

    @pl.when(j == pl.num_programs(1) - 1)
    def _():
        y_ref[...] = _layer_norm(alpha * h_ref[...] + acc_ref[...].T, g2_ref[...], b2_ref[...])


def _peer_mix(h, u, vt, s1, e1, s2, e2, tau, ln2_g, ln2_b, alpha, tm, n_i):
    m, d = h.shape
    n_heads, _, n_keys, _ = s2.shape
    assert m % tm == 0 and tm % LANES == 0 and n_keys % n_i == 0 and n_i % 8 == 0
    tn = n_i * n_keys
    n_c = tm // LANES
    row_stat = pl.BlockSpec((n_heads, n_c, n_i, LANES), lambda i, j: (0, i, j, 0))
    once = pl.Buffered(1)
    full_stat = pl.BlockSpec((n_heads, n_c, n_keys, LANES), lambda i, j: (0, i, 0, 0), pipeline_mode=once)
    return pl.pallas_call(
        functools.partial(_peer_mix_kernel, alpha, n_i),
        grid=(m // tm, n_keys // n_i),
        in_specs=[pl.BlockSpec((tm, d), lambda i, j: (i, 0), pipeline_mode=once),
                  pl.BlockSpec((tn, d), lambda i, j: (j, 0)),
                  pl.BlockSpec((d, tn), lambda i, j: (0, j)),
                  row_stat, row_stat, full_stat, full_stat,
                  pl.BlockSpec((n_c, n_heads, LANES), lambda i, j: (i, 0, 0)),
                  pl.BlockSpec((1, d), lambda i, j: (0, 0)),
                  pl.BlockSpec((1, d), lambda i, j: (0, 0))],
        out_specs=pl.BlockSpec((tm, d), lambda i, j: (i, 0)),
        out_shape=jax.ShapeDtypeStruct((m, d), F32),
        scratch_shapes=[pltpu.VMEM((tm, d), BF16), pltpu.VMEM((n_c, tn, LANES), F32),
                        pltpu.VMEM((n_c, tn, LANES), BF16), pltpu.VMEM((d, tm), F32)],
        compiler_params=_params("parallel", "arbitrary"),
        name="peer_mix",
    )(h, u, vt, s1, e1, s2, e2, tau, ln2_g, ln2_b)


def _pick_tile(m, target):
    t = min(m, target)
    while m % t:
        t //= 2
    return t


def _layer_weights(w_in, mu_shift, w0, w_decay_up, a0, w_aaa_up, w_gate_up, k_k, k_a, r_k,
                   lnx_g, lnx_b, w_out, ln1_g, ln1_b, peer_wq, peer_subkeys, peer_u, peer_v,
                   ln2_g, ln2_b, sb_width):
    rw_width = w0.shape[0]
    d_lora, a_lora, g_lora = w_decay_up.shape[0], w_aaa_up.shape[0], w_gate_up.shape[0]
    in_cols = w_in.shape[1]
    rw_cols = in_cols - 3 * sb_width
    tail = rw_cols - 3 * rw_width
    assert tail == d_lora + a_lora + g_lora
    tail_w = -(-tail // LANES) * LANES
    cols_pad = 3 * sb_width + 3 * rw_width + tail_w
    assert sb_width % rw_width == 0 and (3 * sb_width + 3 * rw_width) % tail_w == 0
    row = lambda x: x.reshape(1, -1).astype(F32)
    mu = jnp.pad(mu_shift, (0, tail_w - tail))

    def lora(w, start):
        return jnp.zeros((tail_w, rw_width), F32).at[start:start + w.shape[0]].set(w).astype(BF16)

    n_ph, _, n_keys, dk = peer_subkeys.shape
    return dict(
        w_in=jnp.pad(w_in, ((0, 0), (0, cols_pad - in_cols))).astype(BF16),
        rw_prm=[row(mu[:rw_width]), row(mu[rw_width:2 * rw_width]), row(mu[2 * rw_width:3 * rw_width]),
                row(mu[3 * rw_width:]), row(w0), row(a0), row(k_k), row(k_a), row(r_k),
                lora(w_decay_up, 0), lora(w_aaa_up, d_lora), lora(w_gate_up, d_lora + a_lora),
                _head_ones()],
        lnx_g=row(lnx_g), lnx_b=row(lnx_b), w_out=w_out.astype(BF16),
        ln1_g=row(ln1_g), ln1_b=row(ln1_b), ln2_g=row(ln2_g), ln2_b=row(ln2_b),
        peer_wq=peer_wq.astype(BF16),
        subkeys=peer_subkeys.reshape(n_ph * 2, n_keys, dk).astype(BF16),
        peer_u=peer_u.astype(BF16), peer_vt=peer_v.astype(BF16).T,
        rw_width=rw_width, rw_cols=rw_cols, tail_w=tail_w, cols_pad=cols_pad,
    )


def _in_proj(x, lw):
    m = x.shape[0]
    n = lw["cols_pad"]
    tn = n // 3 if n % (3 * LANES) == 0 else n
    return _matmul(x, lw["w_in"], _pick_tile(m, 512), tn, "in_proj")


def _peer(h, lw, alpha):
    m = h.shape[0]
    m_pad = -(-m // LANES) * LANES
    hp = jnp.pad(h, ((0, m_pad - m), (0, 0))) if m_pad != m else h
    q = _matmul(hp, lw["peer_wq"], _pick_tile(m_pad, 512), lw["peer_wq"].shape[1], "peer_q")
    s1, e1, s2, e2, tau = _peer_stats(q, lw["subkeys"], _pick_tile(m_pad, 256))
    y = _peer_mix(hp, lw["peer_u"], lw["peer_vt"], s1, e1, s2, e2, tau, lw["ln2_g"], lw["ln2_b"],
                  alpha, _pick_tile(m_pad, 512), 8)
    return y[:m]


def _wkv_to_lanes(s):
    b, h, v, k = s.shape
    return s.transpose(0, 2, 1, 3).reshape(b, v, h * k)


def _wkv_from_lanes(s, n_heads):
    b, v, hk = s.shape
    return s.reshape(b, v, n_heads, hk // n_heads).transpose(0, 2, 1, 3)


def kernel(x_prompt, x_sample, cache_sb_k, cache_sb_v, state_wkv, state_shift, page_table, w_in, sb_bias, mu_shift, w0, w_decay_up, a0, w_aaa_up, w_gate_up, k_k, k_a, r_k, lnx_g, lnx_b, w_out, ln1_g, ln1_b, peer_wq, peer_subkeys, peer_u, peer_v, ln2_g, ln2_b):
    depth = w_in.shape[0]
    bsz, seq, d_model = x_prompt.shape
    dec_b, dec_seq, _ = x_sample.shape
    assert dec_seq == 1, "one new token per sampled sequence"
    _, n_pool, page, sb_heads, head_dim = cache_sb_k.shape
    rw_heads = state_wkv.shape[2]
    assert head_dim == HEAD_DIM and state_wkv.shape[3] == HEAD_DIM
    sb_width = sb_heads * HEAD_DIM
    alpha = (2 * depth) ** 0.25

    h_p = [x_prompt[b] for b in range(bsz)]
    h_s = x_sample[:, 0]
    outs = {name: [] for name in ("kp", "vp", "wp", "sp", "ks", "vs", "ws", "ss")}
    for l in range(depth):
        lw = _layer_weights(w_in[l], mu_shift[l], w0[l], w_decay_up[l], a0[l], w_aaa_up[l], w_gate_up[l],
                            k_k[l], k_a[l], r_k[l].reshape(-1), lnx_g[l], lnx_b[l], w_out[l], ln1_g[l],
                            ln1_b[l], peer_wq[l], peer_subkeys[l], peer_u[l], peer_v[l], ln2_g[l],
                            ln2_b[l], sb_width)
        rw_width, rw_cols, tail_w = lw["rw_width"], lw["rw_cols"], lw["tail_w"]
        rw0 = 3 * sb_width

        kp, vp, wp, sp = [], [], [], []
        for b in range(bsz):
            x = h_p[b]
            p = _in_proj(x, lw)
            kp.append(p[:, sb_width:2 * sb_width].reshape(seq, sb_heads, HEAD_DIM))
            vp.append(p[:, 2 * sb_width:3 * sb_width].reshape(seq, sb_heads, HEAD_DIM))
            sp.append(p[seq - 1:seq, rw0:rw0 + rw_cols])
            zeros = [jnp.zeros((1, rw_width), F32)] * 3 + [jnp.zeros((1, tail_w), F32)]
            r, w, k, v, a, bb, gate, bonus = _rw_prep(p, None, zeros, lw["rw_prm"], _pick_tile(seq, 256),
                                                      rw_width, rw0, tail_w)
            y, s_fin = _rw_scan([u[None] for u in (r, w, k, v, a, bb)],
                                jnp.zeros((1, HEAD_DIM, rw_width), F32), _pick_tile(seq, 128))
            wp.append(_wkv_from_lanes(s_fin, rw_heads)[0])
            tq = _pick_tile(seq, 512)
            sb = _sb_seq(p, sb_bias[l], sb_heads, tq, _pick_tile(tq, 256))
            h = _mix(y[0], bonus, gate, sb, x, lw["w_out"], lw["lnx_g"], lw["lnx_b"], lw["ln1_g"],
                     lw["ln1_b"], alpha, _pick_tile(seq, 256))
            h_p[b] = _peer(h, lw, alpha)
        outs["kp"].append(jnp.stack(kp)); outs["vp"].append(jnp.stack(vp))
        outs["wp"].append(jnp.stack(wp)); outs["sp"].append(jnp.stack(sp))

        p = _in_proj(h_s, lw)
        outs["ks"].append(p[:, sb_width:2 * sb_width].reshape(dec_b, 1, sb_heads, HEAD_DIM))
        outs["vs"].append(p[:, 2 * sb_width:3 * sb_width].reshape(dec_b, 1, sb_heads, HEAD_DIM))
        outs["ss"].append(p[:, None, rw0:rw0 + rw_cols])
        shift = state_shift[l][:, 0]
        prev = [shift[:, :rw_width], shift[:, rw_width:2 * rw_width], shift[:, 2 * rw_width:3 * rw_width],
                jnp.pad(shift[:, 3 * rw_width:], ((0, 0), (0, tail_w - (rw_cols - 3 * rw_width))))]
        r, w, k, v, a, bb, gate, bonus = _rw_prep(p, prev, None, lw["rw_prm"], _pick_tile(dec_b, 256),
                                                  rw_width, rw0, tail_w)
        y, s_fin = _rw_scan([u[:, None] for u in (r, w, k, v, a, bb)], _wkv_to_lanes(state_wkv[l]), 1)
        outs["ws"].append(_wkv_from_lanes(s_fin, rw_heads))
        pool_t = lambda c: c.transpose(0, 2, 3, 1).reshape(n_pool, sb_width, page)
        sb = _sb_tok(p[:, :sb_width], sb_bias[l], pool_t(cache_sb_k[l]), pool_t(cache_sb_v[l]),
                     page_table, _pick_tile(page_table.shape[1], 4))
        h = _mix(y[:, 0], bonus, gate, sb.reshape(dec_b, sb_width), h_s, lw["w_out"], lw["lnx_g"], lw["lnx_b"],
                 lw["ln1_g"], lw["ln1_b"], alpha, _pick_tile(dec_b, 256))
        h_s = _peer(h, lw, alpha)

    stack = lambda name: jnp.stack(outs[name])
    return (jnp.stack(h_p), h_s[:, None], stack("kp"), stack("vp"), stack("wp"), stack("sp"),
            stack("ks"), stack("vs"), stack("ws"), stack("ss"))
```

```python
import functools
import math

import numpy as np
import jax
import jax.numpy as jnp
from jax import lax
from jax.experimental import pallas as pl
from jax.experimental.pallas import tpu as pltpu

F32 = jnp.float32
BF16 = jnp.bfloat16

LANES = 128
HEAD_DIM = 64
SEG = 256
GN_EPS = 64e-5
LN_EPS = 1e-5
TOPK = 16
NEG_INF = float("-inf")
VMEM_LIMIT_BYTES = 56 * 1024 * 1024


def _params(*semantics):
    return pltpu.CompilerParams(dimension_semantics=semantics, vmem_limit_bytes=VMEM_LIMIT_BYTES)


def _head_ones():
    blocks = SEG // HEAD_DIM
    return jnp.asarray(np.kron(np.eye(blocks), np.ones((HEAD_DIM, HEAD_DIM))), dtype=BF16)


def _split_bf16(x):
    hi = x.astype(BF16)
    lo = (x - hi.astype(F32)).astype(BF16)
    return hi, lo


def _head_sum(x, ones):
    outs = []
    for g in range(x.shape[-1] // SEG):
        hi, lo = _split_bf16(x[:, g * SEG:(g + 1) * SEG])
        outs.append(jnp.dot(hi, ones, preferred_element_type=F32)
                    + jnp.dot(lo, ones, preferred_element_type=F32))
    return outs[0] if len(outs) == 1 else jnp.concatenate(outs, axis=-1)


def _softplus(x):
    return jnp.maximum(x, 0.0) + jnp.log1p(jnp.exp(-jnp.abs(x)))


def _softplus_pos(x):
    return jnp.maximum(x, 0.0) + jnp.log(1.0 + jnp.exp(-jnp.abs(x)))


def _layer_norm(x, g, b):
    mu = jnp.mean(x, axis=-1, keepdims=True)
    xc = x - mu
    var = jnp.mean(xc * xc, axis=-1, keepdims=True)
    return xc * lax.rsqrt(var + LN_EPS) * g + b


def _matmul_kernel(x_ref, w_ref, o_ref):
    o_ref[...] = jnp.dot(x_ref[...].astype(BF16), w_ref[...], preferred_element_type=F32)


def _matmul(x, w, tm, tn, name):
    m, k = x.shape
    n = w.shape[1]
    assert m % tm == 0 and n % tn == 0
    return pl.pallas_call(
        _matmul_kernel,
        grid=(n // tn, m // tm),
        in_specs=[pl.BlockSpec((tm, k), lambda j, i: (i, 0)),
                  pl.BlockSpec((k, tn), lambda j, i: (0, j))],
        out_specs=pl.BlockSpec((tm, tn), lambda j, i: (i, j)),
        out_shape=jax.ShapeDtypeStruct((m, n), F32),
        compiler_params=_params("parallel", "parallel"),
        name=name,
    )(x, w)


def _rw_prep_kernel(halo, *refs):
    if halo:
        (pr, pk, pv, pt, hr, hk, hv, ht, sr, sk, sv, st,
         mur, muk, muv, mut, w0, a0, kk_s, ka_s, rk_s, wd, wa, wg, ones_ref,
         r_o, w_o, k_o, v_o, a_o, b_o, g_o, bon_o) = refs
    else:
        (pr, pk, pv, pt, qr, qk, qv, qt,
         mur, muk, muv, mut, w0, a0, kk_s, ka_s, rk_s, wd, wa, wg, ones_ref,
         r_o, w_o, k_o, v_o, a_o, b_o, g_o, bon_o) = refs

    def shifted(x_ref, idx):
        x = x_ref[...]
        if not halo:
            return x, (qr, qk, qv, qt)[idx][...]
        h_ref = (hr, hk, hv, ht)[idx]
        s_ref = (sr, sk, sv, st)[idx]
        first = jnp.where(pl.program_id(0) == 0, s_ref[...], h_ref[7:8, :])
        row = lax.broadcasted_iota(jnp.int32, x.shape, 0)
        return x, jnp.where(row == 0, first, pltpu.roll(x, 1, axis=0))

    def lerp(x_ref, idx, mu_ref):
        x, prev = shifted(x_ref, idx)
        return x + (prev - x) * mu_ref[...]

    ones = ones_ref[...]
    xr = lerp(pr, 0, mur)
    xk = lerp(pk, 1, muk)
    xv = lerp(pv, 2, muv)
    xt = lerp(pt, 3, mut)

    dec_in = jnp.dot(jnp.tanh(xt).astype(BF16), wd[...], preferred_element_type=F32)
    w = -_softplus(-(w0[...] + dec_in)) - 0.5
    decay = jnp.exp(-jnp.exp(w))
    iclr = jax.nn.sigmoid(a0[...] + jnp.dot(xt.astype(BF16), wa[...], preferred_element_type=F32))
    gate = jnp.dot(jax.nn.sigmoid(xt).astype(BF16), wg[...], preferred_element_type=F32)

    kk = xk * kk_s[...]
    norm = jnp.sqrt(_head_sum(kk * kk, ones))
    kk = kk / jnp.maximum(norm, 1e-12)
    k2 = xk * (1.0 + (iclr - 1.0) * ka_s[...])
    bonus = _head_sum(xr * k2 * rk_s[...], ones) * xv

    r_o[...] = xr
    w_o[...] = decay
    k_o[...] = k2
    v_o[...] = xv
    a_o[...] = -kk
    b_o[...] = kk * iclr
    g_o[...] = gate
    bon_o[...] = bonus


def _rw_prep(p, prev, shift_rows, prm, tm, rw_width, sb_cols, tail_w):
    m = p.shape[0]
    assert m % tm == 0
    cb = sb_cols // rw_width
    tb = (sb_cols + 3 * rw_width) // tail_w
    assert cb * rw_width == sb_cols and tb * tail_w == sb_cols + 3 * rw_width
    halo = prev is None

    def col(width, c):
        return pl.BlockSpec((tm, width), lambda i, c=c: (i, c))

    def halo_col(width, c):
        return pl.BlockSpec((8, width), lambda i, c=c: (jnp.maximum(i * (tm // 8) - 1, 0), c))

    def row(width):
        return pl.BlockSpec((1, width), lambda i: (0, 0))

    def whole(shape):
        return pl.BlockSpec(shape, lambda i: (0,) * len(shape))

    main_specs = [col(rw_width, cb), col(rw_width, cb + 1), col(rw_width, cb + 2), col(tail_w, tb)]
    if halo:
        assert tm % 8 == 0
        extra_specs = [halo_col(rw_width, cb), halo_col(rw_width, cb + 1), halo_col(rw_width, cb + 2),
                       halo_col(tail_w, tb), row(rw_width), row(rw_width), row(rw_width), row(tail_w)]
        extra_args = [p, p, p, p] + list(shift_rows)
    else:
        extra_specs = [pl.BlockSpec((tm, rw_width), lambda i: (i, 0))] * 3 + [pl.BlockSpec((tm, tail_w), lambda i: (i, 0))]
        extra_args = list(prev)
    prm_specs = [row(rw_width), row(rw_width), row(rw_width), row(tail_w),
                 row(rw_width), row(rw_width), row(rw_width), row(rw_width), row(rw_width),
                 whole((tail_w, rw_width)), whole((tail_w, rw_width)), whole((tail_w, rw_width)),
                 whole((SEG, SEG))]
    out_spec = pl.BlockSpec((tm, rw_width), lambda i: (i, 0))
    out_shape = jax.ShapeDtypeStruct((m, rw_width), F32)
    return pl.pallas_call(
        functools.partial(_rw_prep_kernel, halo),
        grid=(m // tm,),
        in_specs=main_specs + extra_specs + prm_specs,
        out_specs=[out_spec] * 8,
        out_shape=[out_shape] * 8,
        compiler_params=_params("parallel"),
        name="rw_prep_seq" if halo else "rw_prep_tok",
    )(p, p, p, p, *extra_args, *prm)


def _rw_scan_kernel(r_ref, w_ref, k_ref, v_ref, a_ref, b_ref, s0_ref, ones_ref,
                    y_ref, sT_ref, s_ref, vb_ref):
    tc = r_ref.shape[0]
    width = r_ref.shape[1]
    groups = width // SEG

    @pl.when(pl.program_id(1) == 0)
    def _():
        for g in range(groups):
            s_ref[g] = s0_ref[:, g * SEG:(g + 1) * SEG]

    ones = ones_ref[...]
    lane = lax.broadcasted_iota(jnp.int32, (HEAD_DIM, SEG), 1)
    sub = lax.broadcasted_iota(jnp.int32, (HEAD_DIM, SEG), 0)
    diag = (lane % HEAD_DIM) == sub

    sls = [slice(g * SEG, (g + 1) * SEG) for g in range(groups)]
    n = groups * HEAD_DIM

    diag_next = (lane % HEAD_DIM) == ((sub + 1) % HEAD_DIM)
    first_lane = (lax.broadcasted_iota(jnp.int32, (1, width), 1) % HEAD_DIM) == 0

    def spread_rows(t):
        v = v_ref[pl.ds(t, 1), :]
        v_hi = v.astype(BF16).astype(F32)
        v_lo = v - v_hi
        v_lo = jnp.where(first_lane, pltpu.roll(v_lo, width - (HEAD_DIM - 1), axis=1), pltpu.roll(v_lo, 1, axis=1))
        return [jnp.where(diag, v_hi[:, sl], jnp.where(diag_next, v_lo[:, sl], 0.0)).astype(BF16) for sl in sls]

    def put_vb(res):
        for g in range(groups):
            vb_ref[g] = res[g * HEAD_DIM:(g + 1) * HEAD_DIM]

    put_vb(jnp.dot(jnp.concatenate(spread_rows(0), axis=0), ones, preferred_element_type=F32))

    def step(t, carry):
        r = r_ref[pl.ds(t, 1), :]
        w = w_ref[pl.ds(t, 1), :]
        k = k_ref[pl.ds(t, 1), :]
        a = a_ref[pl.ds(t, 1), :]
        b = b_ref[pl.ds(t, 1), :]
        sa = jnp.dot(jnp.concatenate([(s_ref[g] * a[:, sls[g]]).astype(BF16) for g in range(groups)], axis=0),
                     ones, preferred_element_type=F32)
        outs = []
        for g in range(groups):
            sl = sls[g]
            rows = slice(g * HEAD_DIM, (g + 1) * HEAD_DIM)
            s_new = s_ref[g] * w[:, sl] + sa[rows] * b[:, sl] + vb_ref[g] * k[:, sl]
            s_ref[g] = s_new
            outs.append((s_new * r[:, sl]).astype(BF16))
        res = jnp.dot(jnp.concatenate(outs + spread_rows(jnp.minimum(t + 1, tc - 1)), axis=0), ones,
                      preferred_element_type=F32)
        ys = [jnp.sum(jnp.where(diag, res[g * HEAD_DIM:(g + 1) * HEAD_DIM], 0.0), axis=0, keepdims=True)
              for g in range(groups)]
        y_ref[pl.ds(t, 1), :] = jnp.concatenate(ys, axis=-1)
        put_vb(res[n:2 * n])
        return carry

    lax.fori_loop(0, tc, step, 0, unroll=math.gcd(tc, 8))

    @pl.when(pl.program_id(1) == pl.num_programs(1) - 1)
    def _():
        for g in range(groups):
            sT_ref[:, g * SEG:(g + 1) * SEG] = s_ref[g]


def _rw_scan(seqs, s0, tc):
    bsz, t, width = seqs[0].shape
    assert t % tc == 0 and width % SEG == 0
    seq_spec = pl.BlockSpec((None, tc, width), lambda b, c: (b, c, 0))
    st_spec = pl.BlockSpec((None, HEAD_DIM, width), lambda b, c: (b, 0, 0))
    return pl.pallas_call(
        _rw_scan_kernel,
        grid=(bsz, t // tc),
        in_specs=[seq_spec] * 6 + [st_spec, pl.BlockSpec((SEG, SEG), lambda b, c: (0, 0))],
        out_specs=[seq_spec, st_spec],
        out_shape=[jax.ShapeDtypeStruct((bsz, t, width), F32),
                   jax.ShapeDtypeStruct((bsz, HEAD_DIM, width), F32)],
        scratch_shapes=[pltpu.VMEM((width // SEG, HEAD_DIM, SEG), F32)] * 2,
        compiler_params=_params("parallel", "arbitrary"),
        name="rw_scan",
    )(*seqs, s0, _head_ones())


def _sb_seq_kernel(scale, tk, bias_ref, q_ref, k_ref, v_ref, tri_ref, o_ref, qs_ref, acc_ref, run_ref):
    tq = q_ref.shape[0]
    per_q = tq // tk
    hp = pl.program_id(0)
    qi = pl.program_id(1)
    lane = lax.broadcasted_iota(jnp.int32, (tq, LANES), 1)
    q = q_ref[...] * scale
    qs_ref[0] = jnp.where(lane < HEAD_DIM, q, 0.0).astype(BF16)
    qs_ref[1] = jnp.where(lane >= HEAD_DIM, q, 0.0).astype(BF16)
    acc_ref[...] = jnp.zeros_like(acc_ref)
    run_ref[...] = jnp.zeros_like(run_ref)
    bias = [bias_ref[2 * hp], bias_ref[2 * hp + 1]]

    def block(kb, masked):
        start = pl.multiple_of(kb * tk, tk)
        kblk = k_ref[pl.ds(start, tk), :].astype(BF16)
        vblk = v_ref[pl.ds(start, tk), :].astype(BF16)
        tri = tri_ref[...]
        if masked:
            row = lax.broadcasted_iota(jnp.int32, (tq, tk), 0) + qi * tq
            colm = lax.broadcasted_iota(jnp.int32, (tq, tk), 1) + start
            causal = colm < row
        for s in range(2):
            z = lax.dot_general(qs_ref[s], kblk, (((1,), (1,)), ((), ())),
                                preferred_element_type=F32) + bias[s]
            drop = _softplus_pos(z)
            if masked:
                drop = jnp.where(causal, drop, 0.0)
            later = jnp.dot(drop.astype(BF16), tri, preferred_element_type=F32)
            att = jnp.exp(z - drop + later)
            if masked:
                att = jnp.where(causal, att, 0.0)
            pv = jnp.dot(att.astype(BF16), vblk, preferred_element_type=F32)
            acc_ref[s] += jnp.exp(run_ref[s]) * pv
            run_ref[s] += jnp.broadcast_to(later[:, 0:1] - drop[:, 0:1], (tq, LANES))

    for d in range(per_q):
        block(qi * per_q + per_q - 1 - d, True)

    def body(i, carry):
        for d in range(per_q):
            block((qi - 1 - i) * per_q + per_q - 1 - d, False)
        return carry

    lax.fori_loop(0, qi, body, 0)
    o_ref[...] = jnp.where(lane < HEAD_DIM, acc_ref[0], acc_ref[1])


def _sb_tri(tk):
    j = np.arange(tk)[:, None]
    s = np.arange(tk)[None, :]
    return jnp.asarray(-(j > s).astype(np.float32), dtype=BF16)


def _sb_seq(p, bias, n_heads, tq, tk):
    t = p.shape[0]
    assert t % tq == 0 and tq % tk == 0 and n_heads % 2 == 0
    pairs = n_heads // 2
    scale = 1.0 / math.sqrt(HEAD_DIM)
    return pl.pallas_call(
        functools.partial(_sb_seq_kernel, scale, tk),
        grid=(pairs, t // tq),
        in_specs=[pl.BlockSpec(memory_space=pltpu.SMEM),
                  pl.BlockSpec((tq, LANES), lambda h, i: (i, h)),
                  pl.BlockSpec((t, LANES), lambda h, i: (0, pairs + h)),
                  pl.BlockSpec((t, LANES), lambda h, i: (0, 2 * pairs + h)),
                  pl.BlockSpec((tk, tk), lambda h, i: (0, 0))],
        out_specs=pl.BlockSpec((tq, LANES), lambda h, i: (i, h)),
        out_shape=jax.ShapeDtypeStruct((t, n_heads * HEAD_DIM), F32),
        scratch_shapes=[pltpu.VMEM((2, tq, LANES), BF16), pltpu.VMEM((2, tq, LANES), F32),
                        pltpu.VMEM((2, tq, LANES), F32)],
        compiler_params=_params("parallel", "arbitrary"),
        name="sb_seq",
    )(bias, p, p, p, _sb_tri(tk))


def _sb_tok_kernel(n_pp, scale, pt_ref, bias_ref, q_ref, *refs):
    k_refs = refs[:n_pp]
    v_refs = refs[n_pp:2 * n_pp]
    tri_ref, o_ref, acc_ref, run_ref = refs[2 * n_pp:]
    j = pl.program_id(1)
    width, page = acc_ref.shape
    n_heads = width // HEAD_DIM

    @pl.when(j == 0)
    def _():
        acc_ref[...] = jnp.zeros_like(acc_ref)
        run_ref[...] = jnp.zeros_like(run_ref)

    q = q_ref[...]
    tri = tri_ref[...]
    bias = bias_ref[...]
    for i in range(n_pp):
        prod = k_refs[i][...] * q
        z = jnp.sum(prod.reshape(n_heads, HEAD_DIM, page), axis=1) * scale + bias
        drop = _softplus_pos(z)
        hi, lo = _split_bf16(-drop)
        sums = jnp.dot(jnp.concatenate([hi, lo], axis=1), tri, preferred_element_type=F32)
        att = jnp.exp(z - drop + sums[:, :page] + run_ref[...])
        att_wide = jnp.broadcast_to(att[:, None, :], (n_heads, HEAD_DIM, page)).reshape(width, page)
        acc_ref[...] += att_wide * v_refs[i][...]
        run_ref[...] += sums[:, page:]

    @pl.when(j == pl.num_programs(1) - 1)
    def _():
        o_ref[...] = jnp.sum(acc_ref[...], axis=1, keepdims=True)


def _sb_tok(q, bias, cache_k, cache_v, page_table, n_pp):
    bsz, width = q.shape
    _, _, page = cache_k.shape
    n_pages = page_table.shape[1]
    n_heads = width // HEAD_DIM
    assert n_pages % n_pp == 0
    steps = n_pages // n_pp
    scale = 1.0 / math.sqrt(HEAD_DIM)
    jj = np.arange(page)[:, None]
    ss = np.arange(page)[None, :]
    half = np.concatenate([(jj > ss), np.ones((page, page), bool)], axis=1)
    tri = jnp.asarray(np.concatenate([half, half], axis=0), dtype=BF16)
    bias_rows = jnp.broadcast_to(bias.astype(F32)[:, None], (n_heads, page))
    q_cols = jnp.broadcast_to(q[:, :, None], (bsz, width, page))

    def page_spec(i):
        return pl.BlockSpec((None, width, page),
                            lambda b, j, pt, i=i: (pt[b, n_pages - 1 - (j * n_pp + i)], 0, 0))

    def const(shape):
        return pl.BlockSpec(shape, lambda b, j, pt: (0,) * len(shape))

    grid_spec = pltpu.PrefetchScalarGridSpec(
        num_scalar_prefetch=1,
        grid=(bsz, steps),
        in_specs=[const((n_heads, page)),
                  pl.BlockSpec((None, width, page), lambda b, j, pt: (b, 0, 0))]
                 + [page_spec(i) for i in range(n_pp)] * 2
                 + [const((2 * page, 2 * page))],
        out_specs=pl.BlockSpec((None, width, 1), lambda b, j, pt: (b, 0, 0)),
        scratch_shapes=[pltpu.VMEM((width, page), F32), pltpu.VMEM((n_heads, page), F32)],
    )
    return pl.pallas_call(
        functools.partial(_sb_tok_kernel, n_pp, scale),
        grid_spec=grid_spec,
        out_shape=jax.ShapeDtypeStruct((bsz, width, 1), F32),
        compiler_params=_params("parallel", "arbitrary"),
        name="sb_tok",
    )(page_table, bias_rows, q_cols, *([cache_k] * n_pp), *([cache_v] * n_pp), tri)


def _mix_kernel(alpha, y_ref, bon_ref, gate_ref, sb_ref, x_ref, wo_ref, lg_ref, lb_ref,
                g1_ref, b1_ref, ones_ref, h_ref):
    ones = ones_ref[...]
    rw_width = y_ref.shape[1]
    y = y_ref[...]
    inv_n = 1.0 / HEAD_DIM
    mean = _head_sum(y, ones) * inv_n
    yc = y - mean
    var = _head_sum(yc * yc, ones) * inv_n
    yn = yc * lax.rsqrt(var + GN_EPS) * lg_ref[...] + lb_ref[...]
    rw = (yn + bon_ref[...]) * gate_ref[...]
    mix = (jnp.dot(rw.astype(BF16), wo_ref[0:rw_width, :], preferred_element_type=F32)
           + jnp.dot(sb_ref[...].astype(BF16), wo_ref[rw_width:, :], preferred_element_type=F32))
    h_ref[...] = _layer_norm(alpha * x_ref[...] + mix, g1_ref[...], b1_ref[...])


def _mix(y, bonus, gate, sb, x, wo, lnx_g, lnx_b, ln1_g, ln1_b, alpha, tm):
    m, d = x.shape
    rw_width, sb_width = y.shape[1], sb.shape[1]
    assert m % tm == 0

    def tile(width):
        return pl.BlockSpec((tm, width), lambda i: (i, 0))

    def whole(shape):
        return pl.BlockSpec(shape, lambda i: (0,) * len(shape))

    return pl.pallas_call(
        functools.partial(_mix_kernel, alpha),
        grid=(m // tm,),
        in_specs=[tile(rw_width), tile(rw_width), tile(rw_width), tile(sb_width), tile(d),
                  whole(wo.shape), whole((1, rw_width)), whole((1, rw_width)),
                  whole((1, d)), whole((1, d)), whole((SEG, SEG))],
        out_specs=tile(d),
        out_shape=jax.ShapeDtypeStruct((m, d), F32),
        compiler_params=_params("parallel"),
        name="mix_ln",
    )(y, bonus, gate, sb, x, wo, lnx_g, lnx_b, ln1_g, ln1_b, _head_ones())


def _top_values(x, count):
    rows = lax.broadcasted_iota(jnp.int32, x.shape, 0)
    vals = []
    for _ in range(count):
        m = jnp.max(x, axis=0, keepdims=True)
        vals.append(m)
        first = jnp.min(jnp.where(x == m, rows, x.shape[0]), axis=0, keepdims=True)
        x = jnp.where(rows == first, NEG_INF, x)
    return vals


def _rank_pairs():
    return [(m, n) for m in range(TOPK) for n in range(TOPK) if (m + 1) * (n + 1) <= TOPK]


def _peer_stats_kernel(q_ref, sk_ref, s1_ref, e1_ref, s2_ref, e2_ref, tau_ref, cand_ref):
    n_heads = s1_ref.shape[0]
    dk = sk_ref.shape[2]
    pairs = _rank_pairs()
    cand_ref[...] = jnp.full(cand_ref.shape, NEG_INF, F32)
    for h in range(n_heads):
        scores = []
        for c in range(2):
            qc = q_ref[:, (2 * h + c) * dk:(2 * h + c + 1) * dk].astype(BF16)
            scores.append(lax.dot_general(sk_ref[2 * h + c], qc, (((1,), (1,)), ((), ())),
                                          preferred_element_type=F32))
        top1 = _top_values(scores[0], TOPK)
        top2 = _top_values(scores[1], TOPK)
        for idx, (m, n) in enumerate(pairs):
            cand_ref[idx:idx + 1, :] = top1[m] + top2[n]
        best = _top_values(cand_ref[...], TOPK)
        z = sum(jnp.exp(b - best[0]) for b in best)
        e1 = jnp.exp(scores[0] - top1[0]) / z
        e2 = jnp.exp(scores[1] - top2[0])
        for c in range(s1_ref.shape[1]):
            cols = slice(c * LANES, (c + 1) * LANES)
            s1_ref[h, c] = scores[0][:, cols]
            s2_ref[h, c] = scores[1][:, cols]
            e1_ref[h, c] = e1[:, cols]
            e2_ref[h, c] = e2[:, cols]
            tau_ref[c, h:h + 1, :] = best[TOPK - 1][:, cols]


def _peer_stats(q, subkeys, tb):
    m = q.shape[0]
    n_hc, n_keys, dk = subkeys.shape
    n_heads = n_hc // 2
    assert m % tb == 0 and tb % LANES == 0
    cb = tb // LANES
    stat = pl.BlockSpec((n_heads, cb, n_keys, LANES), lambda i: (0, i, 0, 0))
    stat_shape = jax.ShapeDtypeStruct((n_heads, m // LANES, n_keys, LANES), F32)
    return pl.pallas_call(
        _peer_stats_kernel,
        grid=(m // tb,),
        in_specs=[pl.BlockSpec((tb, n_hc * dk), lambda i: (i, 0)),
                  pl.BlockSpec((n_hc, n_keys, dk), lambda i: (0, 0, 0))],
        out_specs=[stat, stat, stat, stat, pl.BlockSpec((cb, n_heads, LANES), lambda i: (i, 0, 0))],
        out_shape=[stat_shape] * 4 + [jax.ShapeDtypeStruct((m // LANES, n_heads, LANES), F32)],
        scratch_shapes=[pltpu.VMEM((-(-len(_rank_pairs()) // 8) * 8, tb), F32)],
        compiler_params=_params("parallel"),
        name="peer_stats",
    )(q, subkeys)


def _peer_mix_kernel(alpha, n_i, h_ref, u_ref, vt_ref, s1_ref, e1_ref, s2_ref, e2_ref, tau_ref,
                     g2_ref, b2_ref, y_ref, hb_ref, act_ref, coef_ref, acc_ref):
    j = pl.program_id(1)
    n_heads, n_c, n_keys, _ = s2_ref.shape

    @pl.when(j == 0)
    def _():
        hb_ref[...] = h_ref[...].astype(BF16)
        acc_ref[...] = jnp.zeros_like(acc_ref)

    act = lax.dot_general(u_ref[...], hb_ref[...], (((1,), (1,)), ((), ())), preferred_element_type=F32)
    for c in range(n_c):
        act_ref[c] = act[:, c * LANES:(c + 1) * LANES]

    def tile(idx, carry):
        il = idx // n_c
        c = idx % n_c
        rows = pl.ds(pl.multiple_of(il * n_keys, n_keys), n_keys)
        gate = jnp.zeros((n_keys, LANES), F32)
        for h in range(n_heads):
            s1 = s1_ref[h, c, pl.ds(il, 1), :]
            e1 = e1_ref[h, c, pl.ds(il, 1), :]
            tau = tau_ref[c, h:h + 1, :]
            gate = gate + jnp.where(s1 + s2_ref[h, c] >= tau, e2_ref[h, c], 0.0) * e1
        a = act_ref[c, rows, :]
        gelu = 0.5 * a * (1.0 + lax.erf(a * math.sqrt(0.5)))
        coef_ref[c, rows, :] = (gate * gelu).astype(BF16)
        return carry

    lax.fori_loop(0, n_i * n_c, tile, 0)
    coef = jnp.concatenate([coef_ref[c] for c in range(n_c)], axis=1)
    acc_ref[...] += jnp.dot(vt_ref[...], coef, preferred_element_type=F32)

    @pl.when(j == pl.num_programs(1) - 1)
    def _():
        y_ref[...] = _layer_norm(alpha * h_ref[...] + acc_ref[...].T, g2_ref[...], b2_ref[...])


def _peer_mix(h, u, vt, s1, e1, s2, e2, tau, ln2_g, ln2_b, alpha, tm, n_i):
    m, d = h.shape
    n_heads, _, n_keys, _ = s2.shape
    assert m % tm == 0 and tm % LANES == 0 and n_keys % n_i == 0 and n_i % 8 == 0
    tn = n_i * n_keys
    n_c = tm // LANES
    row_stat = pl.BlockSpec((n_heads, n_c, n_i, LANES), lambda i, j: (0, i, j, 0))
    once = pl.Buffered(1)
    full_stat = pl.BlockSpec((n_heads, n_c, n_keys, LANES), lambda i, j: (0, i, 0, 0), pipeline_mode=once)
    return pl.pallas_call(
        functools.partial(_peer_mix_kernel, alpha, n_i),
        grid=(m // tm, n_keys // n_i),
        in_specs=[pl.BlockSpec((tm, d), lambda i, j: (i, 0), pipeline_mode=once),
                  pl.BlockSpec((tn, d), lambda i, j: (j, 0)),
                  pl.BlockSpec((d, tn), lambda i, j: (0, j)),
                  row_stat, row_stat, full_stat, full_stat,
                  pl.BlockSpec((n_c, n_heads, LANES), lambda i, j: (i, 0, 0)),
                  pl.BlockSpec((1, d), lambda i, j: (0, 0)),
                  pl.BlockSpec((1, d), lambda i, j: (0, 0))],
        out_specs=pl.BlockSpec((tm, d), lambda i, j: (i, 0)),
        out_shape=jax.ShapeDtypeStruct((m, d), F32),
        scratch_shapes=[pltpu.VMEM((tm, d), BF16), pltpu.VMEM((n_c, tn, LANES), F32),
                        pltpu.VMEM((n_c, tn, LANES), BF16), pltpu.VMEM((d, tm), F32)],
        compiler_params=_params("parallel", "arbitrary"),
        name="peer_mix",
    )(h, u, vt, s1, e1, s2, e2, tau, ln2_g, ln2_b)


def _pick_tile(m, target):
    t = min(m, target)
    while m % t:
        t //= 2
    return t


def _layer_weights(w_in, mu_shift, w0, w_decay_up, a0, w_aaa_up, w_gate_up, k_k, k_a, r_k,
                   lnx_g, lnx_b, w_out, ln1_g, ln1_b, peer_wq, peer_subkeys, peer_u, peer_v,
                   ln2_g, ln2_b, sb_width):
    rw_width = w0.shape[0]
    d_lora, a_lora, g_lora = w_decay_up.shape[0], w_aaa_up.shape[0], w_gate_up.shape[0]
    in_cols = w_in.shape[1]
    rw_cols = in_cols - 3 * sb_width
    tail = rw_cols - 3 * rw_width
    assert tail == d_lora + a_lora + g_lora
    tail_w = -(-tail // LANES) * LANES
    cols_pad = 3 * sb_width + 3 * rw_width + tail_w
    assert sb_width % rw_width == 0 and (3 * sb_width + 3 * rw_width) % tail_w == 0
    row = lambda x: x.reshape(1, -1).astype(F32)
    mu = jnp.pad(mu_shift, (0, tail_w - tail))

    def lora(w, start):
        return jnp.zeros((tail_w, rw_width), F32).at[start:start + w.shape[0]].set(w).astype(BF16)

    n_ph, _, n_keys, dk = peer_subkeys.shape
    return dict(
        w_in=jnp.pad(w_in, ((0, 0), (0, cols_pad - in_cols))).astype(BF16),
        rw_prm=[row(mu[:rw_width]), row(mu[rw_width:2 * rw_width]), row(mu[2 * rw_width:3 * rw_width]),
                row(mu[3 * rw_width:]), row(w0), row(a0), row(k_k), row(k_a), row(r_k),
                lora(w_decay_up, 0), lora(w_aaa_up, d_lora), lora(w_gate_up, d_lora + a_lora),
                _head_ones()],
        lnx_g=row(lnx_g), lnx_b=row(lnx_b), w_out=w_out.astype(BF16),
        ln1_g=row(ln1_g), ln1_b=row(ln1_b), ln2_g=row(ln2_g), ln2_b=row(ln2_b),
        peer_wq=peer_wq.astype(BF16),
        subkeys=peer_subkeys.reshape(n_ph * 2, n_keys, dk).astype(BF16),
        peer_u=peer_u.astype(BF16), peer_vt=peer_v.astype(BF16).T,
        rw_width=rw_width, rw_cols=rw_cols, tail_w=tail_w, cols_pad=cols_pad,
    )


def _in_proj(x, lw):
    m = x.shape[0]
    n = lw["cols_pad"]
    tn = n // 3 if n % (3 * LANES) == 0 else n
    return _matmul(x, lw["w_in"], _pick_tile(m, 512), tn, "in_proj")


def _peer(h, lw, alpha):
    m = h.shape[0]
    m_pad = -(-m // LANES) * LANES
    hp = jnp.pad(h, ((0, m_pad - m), (0, 0))) if m_pad != m else h
    q = _matmul(hp, lw["peer_wq"], _pick_tile(m_pad, 512), lw["peer_wq"].shape[1], "peer_q")
    s1, e1, s2, e2, tau = _peer_stats(q, lw["subkeys"], _pick_tile(m_pad, 256))
    y = _peer_mix(hp, lw["peer_u"], lw["peer_vt"], s1, e1, s2, e2, tau, lw["ln2_g"], lw["ln2_b"],
                  alpha, _pick_tile(m_pad, 512), 8)
    return y[:m]


def _wkv_to_lanes(s):
    b, h, v, k = s.shape
    return s.transpose(0, 2, 1, 3).reshape(b, v, h * k)


def _wkv_from_lanes(s, n_heads):
    b, v, hk = s.shape
    return s.reshape(b, v, n_heads, hk // n_heads).transpose(0, 2, 1, 3)


def kernel(x_prompt, x_sample, cache_sb_k, cache_sb_v, state_wkv, state_shift, page_table, w_in, sb_bias, mu_shift, w0, w_decay_up, a0, w_aaa_up, w_gate_up, k_k, k_a, r_k, lnx_g, lnx_b, w_out, ln1_g, ln1_b, peer_wq, peer_subkeys, peer_u, peer_v, ln2_g, ln2_b):
    depth = w_in.shape[0]
    bsz, seq, d_model = x_prompt.shape
    dec_b, dec_seq, _ = x_sample.shape
    assert dec_seq == 1, "one new token per sampled sequence"
    _, n_pool, page, sb_heads, head_dim = cache_sb_k.shape
    rw_heads = state_wkv.shape[2]
    assert head_dim == HEAD_DIM and state_wkv.shape[3] == HEAD_DIM
    sb_width = sb_heads * HEAD_DIM
    alpha = (2 * depth) ** 0.25

    h_p = [x_prompt[b] for b in range(bsz)]
    h_s = x_sample[:, 0]
    outs = {name: [] for name in ("kp", "vp", "wp", "sp", "ks", "vs", "ws", "ss")}
    for l in range(depth):
        lw = _layer_weights(w_in[l], mu_shift[l], w0[l], w_decay_up[l], a0[l], w_aaa_up[l], w_gate_up[l],
                            k_k[l], k_a[l], r_k[l].reshape(-1), lnx_g[l], lnx_b[l], w_out[l], ln1_g[l],
                            ln1_b[l], peer_wq[l], peer_subkeys[l], peer_u[l], peer_v[l], ln2_g[l],
                            ln2_b[l], sb_width)
        rw_width, rw_cols, tail_w = lw["rw_width"], lw["rw_cols"], lw["tail_w"]
        rw0 = 3 * sb_width

        kp, vp, wp, sp = [], [], [], []
        for b in range(bsz):
            x = h_p[b]
            p = _in_proj(x, lw)
            kp.append(p[:, sb_width:2 * sb_width].reshape(seq, sb_heads, HEAD_DIM))
            vp.append(p[:, 2 * sb_width:3 * sb_width].reshape(seq, sb_heads, HEAD_DIM))
            sp.append(p[seq - 1:seq, rw0:rw0 + rw_cols])
            zeros = [jnp.zeros((1, rw_width), F32)] * 3 + [jnp.zeros((1, tail_w), F32)]
            r, w, k, v, a, bb, gate, bonus = _rw_prep(p, None, zeros, lw["rw_prm"], _pick_tile(seq, 256),
                                                      rw_width, rw0, tail_w)
            y, s_fin = _rw_scan([u[None] for u in (r, w, k, v, a, bb)],
                                jnp.zeros((1, HEAD_DIM, rw_width), F32), _pick_tile(seq, 128))
            wp.append(_wkv_from_lanes(s_fin, rw_heads)[0])
            tq = _pick_tile(seq, 512)
            sb = _sb_seq(p, sb_bias[l], sb_heads, tq, _pick_tile(tq, 256))
            h = _mix(y[0], bonus, gate, sb, x, lw["w_out"], lw["lnx_g"], lw["lnx_b"], lw["ln1_g"],
                     lw["ln1_b"], alpha, _pick_tile(seq, 256))
            h_p[b] = _peer(h, lw, alpha)
        outs["kp"].append(jnp.stack(kp)); outs["vp"].append(jnp.stack(vp))
        outs["wp"].append(jnp.stack(wp)); outs["sp"].append(jnp.stack(sp))

        p = _in_proj(h_s, lw)
        outs["ks"].append(p[:, sb_width:2 * sb_width].reshape(dec_b, 1, sb_heads, HEAD_DIM))
        outs["vs"].append(p[:, 2 * sb_width:3 * sb_width].reshape(dec_b, 1, sb_heads, HEAD_DIM))
        outs["ss"].append(p[:, None, rw0:rw0 + rw_cols])
        shift = state_shift[l][:, 0]
        prev = [shift[:, :rw_width], shift[:, rw_width:2 * rw_width], shift[:, 2 * rw_width:3 * rw_width],
                jnp.pad(shift[:, 3 * rw_width:], ((0, 0), (0, tail_w - (rw_cols - 3 * rw_width))))]
        r, w, k, v, a, bb, gate, bonus = _rw_prep(p, prev, None, lw["rw_prm"], _pick_tile(dec_b, 256),
                                                  rw_width, rw0, tail_w)
        y, s_fin = _rw_scan([u[:, None] for u in (r, w, k, v, a, bb)], _wkv_to_lanes(state_wkv[l]), 1)
        outs["ws"].append(_wkv_from_lanes(s_fin, rw_heads))
        pool_t = lambda c: c.transpose(0, 2, 3, 1).reshape(n_pool, sb_width, page)
        sb = _sb_tok(p[:, :sb_width], sb_bias[l], pool_t(cache_sb_k[l]), pool_t(cache_sb_v[l]),
                     page_table, _pick_tile(page_table.shape[1], 8))
        h = _mix(y[:, 0], bonus, gate, sb.reshape(dec_b, sb_width), h_s, lw["w_out"], lw["lnx_g"], lw["lnx_b"],
                 lw["ln1_g"], lw["ln1_b"], alpha, _pick_tile(dec_b, 256))
        h_s = _peer(h, lw, alpha)

    stack = lambda name: jnp.stack(outs[name])
    return (jnp.stack(h_p), h_s[:, None], stack("kp"), stack("vp"), stack("wp"), stack("sp"),
            stack("ks"), stack("vs"), stack("ws"), stack("ss"))
```

```python
import functools
import math

import numpy as np
import jax
import jax.numpy as jnp
from jax import lax
from jax.experimental import pallas as pl
from jax.experimental.pallas import tpu as pltpu

F32 = jnp.float32
BF16 = jnp.bfloat16

LANES = 128
HEAD_DIM = 64
SEG = 256
GN_EPS = 64e-5
LN_EPS = 1e-5
TOPK = 16
NEG_INF = float("-inf")
VMEM_LIMIT_BYTES = 56 * 1024 * 1024


def _params(*semantics):
    return pltpu.CompilerParams(dimension_semantics=semantics, vmem_limit_bytes=VMEM_LIMIT_BYTES)


def _head_ones():
    blocks = SEG // HEAD_DIM
    return jnp.asarray(np.kron(np.eye(blocks), np.ones((HEAD_DIM, HEAD_DIM))), dtype=BF16)


def _split_bf16(x):
    hi = x.astype(BF16)
    lo = (x - hi.astype(F32)).astype(BF16)
    return hi, lo


def _head_sum(x, ones):
    outs = []
    for g in range(x.shape[-1] // SEG):
        hi, lo = _split_bf16(x[:, g * SEG:(g + 1) * SEG])
        outs.append(jnp.dot(hi, ones, preferred_element_type=F32)
                    + jnp.dot(lo, ones, preferred_element_type=F32))
    return outs[0] if len(outs) == 1 else jnp.concatenate(outs, axis=-1)


def _softplus(x):
    return jnp.maximum(x, 0.0) + jnp.log1p(jnp.exp(-jnp.abs(x)))


def _softplus_pos(x):
    return jnp.maximum(x, 0.0) + jnp.log(1.0 + jnp.exp(-jnp.abs(x)))


def _layer_norm(x, g, b):
    mu = jnp.mean(x, axis=-1, keepdims=True)
    xc = x - mu
    var = jnp.mean(xc * xc, axis=-1, keepdims=True)
    return xc * lax.rsqrt(var + LN_EPS) * g + b


def _matmul_kernel(x_ref, w_ref, o_ref):
    o_ref[...] = jnp.dot(x_ref[...].astype(BF16), w_ref[...], preferred_element_type=F32)


def _matmul(x, w, tm, tn, name):
    m, k = x.shape
    n = w.shape[1]
    assert m % tm == 0 and n % tn == 0
    return pl.pallas_call(
        _matmul_kernel,
        grid=(n // tn, m // tm),
        in_specs=[pl.BlockSpec((tm, k), lambda j, i: (i, 0)),
                  pl.BlockSpec((k, tn), lambda j, i: (0, j))],
        out_specs=pl.BlockSpec((tm, tn), lambda j, i: (i, j)),
        out_shape=jax.ShapeDtypeStruct((m, n), F32),
        compiler_params=_params("parallel", "parallel"),
        name=name,
    )(x, w)


def _rw_prep_kernel(halo, *refs):
    if halo:
        (pr, pk, pv, pt, hr, hk, hv, ht, sr, sk, sv, st,
         mur, muk, muv, mut, w0, a0, kk_s, ka_s, rk_s, wd, wa, wg, ones_ref,
         r_o, w_o, k_o, v_o, a_o, b_o, g_o, bon_o) = refs
    else:
        (pr, pk, pv, pt, qr, qk, qv, qt,
         mur, muk, muv, mut, w0, a0, kk_s, ka_s, rk_s, wd, wa, wg, ones_ref,
         r_o, w_o, k_o, v_o, a_o, b_o, g_o, bon_o) = refs

    def shifted(x_ref, idx):
        x = x_ref[...]
        if not halo:
            return x, (qr, qk, qv, qt)[idx][...]
        h_ref = (hr, hk, hv, ht)[idx]
        s_ref = (sr, sk, sv, st)[idx]
        first = jnp.where(pl.program_id(0) == 0, s_ref[...], h_ref[7:8, :])
        row = lax.broadcasted_iota(jnp.int32, x.shape, 0)
        return x, jnp.where(row == 0, first, pltpu.roll(x, 1, axis=0))

    def lerp(x_ref, idx, mu_ref):
        x, prev = shifted(x_ref, idx)
        return x + (prev - x) * mu_ref[...]

    ones = ones_ref[...]
    xr = lerp(pr, 0, mur)
    xk = lerp(pk, 1, muk)
    xv = lerp(pv, 2, muv)
    xt = lerp(pt, 3, mut)

    dec_in = jnp.dot(jnp.tanh(xt).astype(BF16), wd[...], preferred_element_type=F32)
    w = -_softplus(-(w0[...] + dec_in)) - 0.5
    decay = jnp.exp(-jnp.exp(w))
    iclr = jax.nn.sigmoid(a0[...] + jnp.dot(xt.astype(BF16), wa[...], preferred_element_type=F32))
    gate = jnp.dot(jax.nn.sigmoid(xt).astype(BF16), wg[...], preferred_element_type=F32)

    kk = xk * kk_s[...]
    norm = jnp.sqrt(_head_sum(kk * kk, ones))
    kk = kk / jnp.maximum(norm, 1e-12)
    k2 = xk * (1.0 + (iclr - 1.0) * ka_s[...])
    bonus = _head_sum(xr * k2 * rk_s[...], ones) * xv

    r_o[...] = xr
    w_o[...] = decay
    k_o[...] = k2
    v_o[...] = xv
    a_o[...] = -kk
    b_o[...] = kk * iclr
    g_o[...] = gate
    bon_o[...] = bonus


def _rw_prep(p, prev, shift_rows, prm, tm, rw_width, sb_cols, tail_w):
    m = p.shape[0]
    assert m % tm == 0
    cb = sb_cols // rw_width
    tb = (sb_cols + 3 * rw_width) // tail_w
    assert cb * rw_width == sb_cols and tb * tail_w == sb_cols + 3 * rw_width
    halo = prev is None

    def col(width, c):
        return pl.BlockSpec((tm, width), lambda i, c=c: (i, c))

    def halo_col(width, c):
        return pl.BlockSpec((8, width), lambda i, c=c: (jnp.maximum(i * (tm // 8) - 1, 0), c))

    def row(width):
        return pl.BlockSpec((1, width), lambda i: (0, 0))

    def whole(shape):
        return pl.BlockSpec(shape, lambda i: (0,) * len(shape))

    main_specs = [col(rw_width, cb), col(rw_width, cb + 1), col(rw_width, cb + 2), col(tail_w, tb)]
    if halo:
        assert tm % 8 == 0
        extra_specs = [halo_col(rw_width, cb), halo_col(rw_width, cb + 1), halo_col(rw_width, cb + 2),
                       halo_col(tail_w, tb), row(rw_width), row(rw_width), row(rw_width), row(tail_w)]
        extra_args = [p, p, p, p] + list(shift_rows)
    else:
        extra_specs = [pl.BlockSpec((tm, rw_width), lambda i: (i, 0))] * 3 + [pl.BlockSpec((tm, tail_w), lambda i: (i, 0))]
        extra_args = list(prev)
    prm_specs = [row(rw_width), row(rw_width), row(rw_width), row(tail_w),
                 row(rw_width), row(rw_width), row(rw_width), row(rw_width), row(rw_width),
                 whole((tail_w, rw_width)), whole((tail_w, rw_width)), whole((tail_w, rw_width)),
                 whole((SEG, SEG))]
    out_spec = pl.BlockSpec((tm, rw_width), lambda i: (i, 0))
    out_shape = jax.ShapeDtypeStruct((m, rw_width), F32)
    return pl.pallas_call(
        functools.partial(_rw_prep_kernel, halo),
        grid=(m // tm,),
        in_specs=main_specs + extra_specs + prm_specs,
        out_specs=[out_spec] * 8,
        out_shape=[out_shape] * 8,
        compiler_params=_params("parallel"),
        name="rw_prep_seq" if halo else "rw_prep_tok",
    )(p, p, p, p, *extra_args, *prm)


def _rw_scan_kernel(*refs):
    chunk = pl.program_id(1)
    _rw_scan_body(chunk == 0, chunk == pl.num_programs(1) - 1, *refs)


def _rw_scan_body(first, last, r_ref, w_ref, k_ref, v_ref, a_ref, b_ref, s0_ref, ones_ref,
                  y_ref, sT_ref, s_ref, vb_ref):
    tc = r_ref.shape[0]
    width = r_ref.shape[1]
    groups = width // SEG

    @pl.when(first)
    def _():
        for g in range(groups):
            s_ref[g] = s0_ref[:, g * SEG:(g + 1) * SEG]

    ones = ones_ref[...]
    lane = lax.broadcasted_iota(jnp.int32, (HEAD_DIM, SEG), 1)
    sub = lax.broadcasted_iota(jnp.int32, (HEAD_DIM, SEG), 0)
    diag = (lane % HEAD_DIM) == sub

    sls = [slice(g * SEG, (g + 1) * SEG) for g in range(groups)]
    n = groups * HEAD_DIM

    diag_next = (lane % HEAD_DIM) == ((sub + 1) % HEAD_DIM)
    first_lane = (lax.broadcasted_iota(jnp.int32, (1, width), 1) % HEAD_DIM) == 0

    def spread_rows(t):
        v = v_ref[pl.ds(t, 1), :]
        v_hi = v.astype(BF16).astype(F32)
        v_lo = v - v_hi
        v_lo = jnp.where(first_lane, pltpu.roll(v_lo, width - (HEAD_DIM - 1), axis=1), pltpu.roll(v_lo, 1, axis=1))
        return [jnp.where(diag, v_hi[:, sl], jnp.where(diag_next, v_lo[:, sl], 0.0)).astype(BF16) for sl in sls]

    def put_vb(res):
        for g in range(groups):
            vb_ref[g] = res[g * HEAD_DIM:(g + 1) * HEAD_DIM]

    put_vb(jnp.dot(jnp.concatenate(spread_rows(0), axis=0), ones, preferred_element_type=F32))

    def step(t, carry):
        r = r_ref[pl.ds(t, 1), :]
        w = w_ref[pl.ds(t, 1), :]
        k = k_ref[pl.ds(t, 1), :]
        a = a_ref[pl.ds(t, 1), :]
        b = b_ref[pl.ds(t, 1), :]
        sa = jnp.dot(jnp.concatenate([(s_ref[g] * a[:, sls[g]]).astype(BF16) for g in range(groups)], axis=0),
                     ones, preferred_element_type=F32)
        outs = []
        for g in range(groups):
            sl = sls[g]
            rows = slice(g * HEAD_DIM, (g + 1) * HEAD_DIM)
            s_new = s_ref[g] * w[:, sl] + sa[rows] * b[:, sl] + vb_ref[g] * k[:, sl]
            s_ref[g] = s_new
            outs.append((s_new * r[:, sl]).astype(BF16))
        res = jnp.dot(jnp.concatenate(outs + spread_rows(jnp.minimum(t + 1, tc - 1)), axis=0), ones,
                      preferred_element_type=F32)
        ys = [jnp.sum(jnp.where(diag, res[g * HEAD_DIM:(g + 1) * HEAD_DIM], 0.0), axis=0, keepdims=True)
              for g in range(groups)]
        y_ref[pl.ds(t, 1), :] = jnp.concatenate(ys, axis=-1)
        put_vb(res[n:2 * n])
        return carry

    lax.fori_loop(0, tc, step, 0, unroll=tc if tc <= 16 else math.gcd(tc, 8))

    @pl.when(last)
    def _():
        for g in range(groups):
            sT_ref[:, g * SEG:(g + 1) * SEG] = s_ref[g]


def _rw_scan(seqs, s0, tc):
    bsz, t, width = seqs[0].shape
    assert t % tc == 0 and width % SEG == 0
    seq_spec = pl.BlockSpec((None, tc, width), lambda b, c: (b, c, 0))
    st_spec = pl.BlockSpec((None, HEAD_DIM, width), lambda b, c: (b, 0, 0))
    return pl.pallas_call(
        _rw_scan_kernel,
        grid=(bsz, t // tc),
        in_specs=[seq_spec] * 6 + [st_spec, pl.BlockSpec((SEG, SEG), lambda b, c: (0, 0))],
        out_specs=[seq_spec, st_spec],
        out_shape=[jax.ShapeDtypeStruct((bsz, t, width), F32),
                   jax.ShapeDtypeStruct((bsz, HEAD_DIM, width), F32)],
        scratch_shapes=[pltpu.VMEM((width // SEG, HEAD_DIM, SEG), F32)] * 2,
        compiler_params=_params("parallel", "arbitrary"),
        name="rw_scan",
    )(*seqs, s0, _head_ones())


def _sb_seq_kernel(scale, tk, bias_ref, q_ref, k_ref, v_ref, tri_ref, o_ref, qs_ref, acc_ref, run_ref):
    tq = q_ref.shape[0]
    per_q = tq // tk
    hp = pl.program_id(0)
    qi = pl.program_id(1)
    lane = lax.broadcasted_iota(jnp.int32, (tq, LANES), 1)
    q = q_ref[...] * scale
    qs_ref[0] = jnp.where(lane < HEAD_DIM, q, 0.0).astype(BF16)
    qs_ref[1] = jnp.where(lane >= HEAD_DIM, q, 0.0).astype(BF16)
    acc_ref[...] = jnp.zeros_like(acc_ref)
    run_ref[...] = jnp.zeros_like(run_ref)
    bias = [bias_ref[2 * hp], bias_ref[2 * hp + 1]]

    def block(kb, masked):
        start = pl.multiple_of(kb * tk, tk)
        kblk = k_ref[pl.ds(start, tk), :].astype(BF16)
        vblk = v_ref[pl.ds(start, tk), :].astype(BF16)
        tri = tri_ref[...]
        if masked:
            row = lax.broadcasted_iota(jnp.int32, (tq, tk), 0) + qi * tq
            colm = lax.broadcasted_iota(jnp.int32, (tq, tk), 1) + start
            causal = colm < row
        for s in range(2):
            z = lax.dot_general(qs_ref[s], kblk, (((1,), (1,)), ((), ())),
                                preferred_element_type=F32) + bias[s]
            drop = _softplus_pos(z)
            if masked:
                drop = jnp.where(causal, drop, 0.0)
            later = jnp.dot(drop.astype(BF16), tri, preferred_element_type=F32)
            att = jnp.exp(z - drop + later)
            if masked:
                att = jnp.where(causal, att, 0.0)
            pv = jnp.dot(att.astype(BF16), vblk, preferred_element_type=F32)
            acc_ref[s] += jnp.exp(run_ref[s]) * pv
            run_ref[s] += jnp.broadcast_to(later[:, 0:1] - drop[:, 0:1], (tq, LANES))

    for d in range(per_q):
        block(qi * per_q + per_q - 1 - d, True)

    def body(i, carry):
        for d in range(per_q):
            block((qi - 1 - i) * per_q + per_q - 1 - d, False)
        return carry

    lax.fori_loop(0, qi, body, 0)
    o_ref[...] = jnp.where(lane < HEAD_DIM, acc_ref[0], acc_ref[1])


def _sb_tri(tk):
    j = np.arange(tk)[:, None]
    s = np.arange(tk)[None, :]
    return jnp.asarray(-(j > s).astype(np.float32), dtype=BF16)


def _sb_seq(p, bias, n_heads, tq, tk):
    t = p.shape[0]
    assert t % tq == 0 and tq % tk == 0 and n_heads % 2 == 0
    pairs = n_heads // 2
    scale = 1.0 / math.sqrt(HEAD_DIM)
    return pl.pallas_call(
        functools.partial(_sb_seq_kernel, scale, tk),
        grid=(pairs, t // tq),
        in_specs=[pl.BlockSpec(memory_space=pltpu.SMEM),
                  pl.BlockSpec((tq, LANES), lambda h, i: (i, h)),
                  pl.BlockSpec((t, LANES), lambda h, i: (0, pairs + h)),
                  pl.BlockSpec((t, LANES), lambda h, i: (0, 2 * pairs + h)),
                  pl.BlockSpec((tk, tk), lambda h, i: (0, 0))],
        out_specs=pl.BlockSpec((tq, LANES), lambda h, i: (i, h)),
        out_shape=jax.ShapeDtypeStruct((t, n_heads * HEAD_DIM), F32),
        scratch_shapes=[pltpu.VMEM((2, tq, LANES), BF16), pltpu.VMEM((2, tq, LANES), F32),
                        pltpu.VMEM((2, tq, LANES), F32)],
        compiler_params=_params("parallel", "arbitrary"),
        name="sb_seq",
    )(bias, p, p, p, _sb_tri(tk))


def _sb_tok_kernel(n_pp, scale, pt_ref, *refs):
    j = pl.program_id(1)
    _sb_tok_body(n_pp, scale, j == 0, j == pl.num_programs(1) - 1, *refs)


def _sb_tok_body(n_pp, scale, first, last, bias_ref, q_ref, *refs):
    k_refs = refs[:n_pp]
    v_refs = refs[n_pp:2 * n_pp]
    tri_ref, o_ref, acc_ref, run_ref = refs[2 * n_pp:]
    width, page = acc_ref.shape
    n_heads = width // HEAD_DIM

    @pl.when(first)
    def _():
        acc_ref[...] = jnp.zeros_like(acc_ref)
        run_ref[...] = jnp.zeros_like(run_ref)

    q = q_ref[...]
    tri = tri_ref[...]
    bias = bias_ref[...]
    for i in range(n_pp):
        prod = k_refs[i][...] * q
        z = jnp.sum(prod.reshape(n_heads, HEAD_DIM, page), axis=1) * scale + bias
        drop = _softplus_pos(z)
        hi, lo = _split_bf16(-drop)
        sums = jnp.dot(jnp.concatenate([hi, lo], axis=1), tri, preferred_element_type=F32)
        att = jnp.exp(z - drop + sums[:, :page] + run_ref[...])
        att_wide = jnp.broadcast_to(att[:, None, :], (n_heads, HEAD_DIM, page)).reshape(width, page)
        acc_ref[...] += att_wide * v_refs[i][...]
        run_ref[...] += sums[:, page:]

    @pl.when(last)
    def _():
        o_ref[...] = jnp.sum(acc_ref[...], axis=1, keepdims=True)


def _scan_tok_kernel(n_pp, scale, steps, n_tok_in, pt_ref, *refs):
    s = pl.program_id(0)
    j = s % steps
    tok_in = refs[:n_tok_in]
    scan_in = refs[n_tok_in:n_tok_in + 8]
    o_tok, y_ref, sT_ref, acc_ref, run_ref, s_ref, vb_ref = refs[n_tok_in + 8:]
    _rw_scan_body(s == 0, s == pl.num_programs(0) - 1, *scan_in, y_ref, sT_ref, s_ref, vb_ref)
    _sb_tok_body(n_pp, scale, j == 0, j == steps - 1, *tok_in, o_tok, acc_ref, run_ref)


def _sb_tok(q, bias, cache_k, cache_v, page_table, n_pp, scan=None):
    bsz, width = q.shape
    _, _, page = cache_k.shape
    n_pages = page_table.shape[1]
    n_heads = width // HEAD_DIM
    assert n_pages % n_pp == 0
    steps = n_pages // n_pp
    scale = 1.0 / math.sqrt(HEAD_DIM)
    jj = np.arange(page)[:, None]
    ss = np.arange(page)[None, :]
    half = np.concatenate([(jj > ss), np.ones((page, page), bool)], axis=1)
    tri = jnp.asarray(np.concatenate([half, half], axis=0), dtype=BF16)
    bias_rows = jnp.broadcast_to(bias.astype(F32)[:, None], (n_heads, page))
    q_cols = jnp.broadcast_to(q[:, :, None], (bsz, width, page))
    tok_args = [bias_rows, q_cols] + [cache_k] * n_pp + [cache_v] * n_pp + [tri]
    tok_scratch = [pltpu.VMEM((width, page), F32), pltpu.VMEM((n_heads, page), F32)]
    tok_shape = jax.ShapeDtypeStruct((bsz, width, 1), F32)

    def tok_specs(split):
        def page_spec(i):
            def index(*idx):
                b, j = split(*idx[:-1])
                return (idx[-1][b, n_pages - 1 - (j * n_pp + i)], 0, 0)
            return pl.BlockSpec((None, width, page), index)

        def const(shape):
            return pl.BlockSpec(shape, lambda *idx: (0,) * len(shape))

        per_seq = lambda last_dim: pl.BlockSpec((None, width, last_dim), lambda *idx: (split(*idx[:-1])[0], 0, 0))
        ins = ([const((n_heads, page)), per_seq(page)] + [page_spec(i) for i in range(n_pp)] * 2
               + [const((2 * page, 2 * page))])
        return ins, per_seq(1)

    if scan is None:
        ins, out = tok_specs(lambda b, j: (b, j))
        return pl.pallas_call(
            functools.partial(_sb_tok_kernel, n_pp, scale),
            grid_spec=pltpu.PrefetchScalarGridSpec(num_scalar_prefetch=1, grid=(bsz, steps), in_specs=ins,
                                                   out_specs=out, scratch_shapes=tok_scratch),
            out_shape=tok_shape,
            compiler_params=_params("parallel", "arbitrary"),
            name="sb_tok",
        )(page_table, *tok_args)

    seqs, s0 = scan
    _, t, rw_width = seqs[0].shape
    n_steps = bsz * steps
    tc = t // n_steps
    assert tc * n_steps == t and tc % 8 == 0 and seqs[0].shape[0] == 1
    ins, out = tok_specs(lambda s: (s // steps, s % steps))
    seq_spec = pl.BlockSpec((None, tc, rw_width), lambda s, pt: (0, s, 0))
    st_spec = pl.BlockSpec((None, HEAD_DIM, rw_width), lambda s, pt: (0, 0, 0))
    state = pltpu.VMEM((rw_width // SEG, HEAD_DIM, SEG), F32)
    return pl.pallas_call(
        functools.partial(_scan_tok_kernel, n_pp, scale, steps, len(ins)),
        grid_spec=pltpu.PrefetchScalarGridSpec(
            num_scalar_prefetch=1, grid=(n_steps,),
            in_specs=ins + [seq_spec] * 6 + [st_spec, pl.BlockSpec((SEG, SEG), lambda s, pt: (0, 0))],
            out_specs=[out, seq_spec, st_spec],
            scratch_shapes=tok_scratch + [state, state]),
        out_shape=[tok_shape, jax.ShapeDtypeStruct((1, t, rw_width), F32),
                   jax.ShapeDtypeStruct((1, HEAD_DIM, rw_width), F32)],
        compiler_params=_params("arbitrary"),
        name="scan_tok",
    )(page_table, *tok_args, *seqs, s0, _head_ones())


def _mix_kernel(alpha, y_ref, bon_ref, gate_ref, sb_ref, x_ref, wo_ref, lg_ref, lb_ref,
                g1_ref, b1_ref, ones_ref, h_ref):
    ones = ones_ref[...]
    rw_width = y_ref.shape[1]
    y = y_ref[...]
    inv_n = 1.0 / HEAD_DIM
    mean = _head_sum(y, ones) * inv_n
    yc = y - mean
    var = _head_sum(yc * yc, ones) * inv_n
    yn = yc * lax.rsqrt(var + GN_EPS) * lg_ref[...] + lb_ref[...]
    rw = (yn + bon_ref[...]) * gate_ref[...]
    mix = (jnp.dot(rw.astype(BF16), wo_ref[0:rw_width, :], preferred_element_type=F32)
           + jnp.dot(sb_ref[...].astype(BF16), wo_ref[rw_width:, :], preferred_element_type=F32))
    h_ref[...] = _layer_norm(alpha * x_ref[...] + mix, g1_ref[...], b1_ref[...])


def _mix(y, bonus, gate, sb, x, wo, lnx_g, lnx_b, ln1_g, ln1_b, alpha, tm):
    m, d = x.shape
    rw_width, sb_width = y.shape[1], sb.shape[1]
    assert m % tm == 0

    def tile(width):
        return pl.BlockSpec((tm, width), lambda i: (i, 0))

    def whole(shape):
        return pl.BlockSpec(shape, lambda i: (0,) * len(shape))

    return pl.pallas_call(
        functools.partial(_mix_kernel, alpha),
        grid=(m // tm,),
        in_specs=[tile(rw_width), tile(rw_width), tile(rw_width), tile(sb_width), tile(d),
                  whole(wo.shape), whole((1, rw_width)), whole((1, rw_width)),
                  whole((1, d)), whole((1, d)), whole((SEG, SEG))],
        out_specs=tile(d),
        out_shape=jax.ShapeDtypeStruct((m, d), F32),
        compiler_params=_params("parallel"),
        name="mix_ln",
    )(y, bonus, gate, sb, x, wo, lnx_g, lnx_b, ln1_g, ln1_b, _head_ones())


def _top_values(x, count):
    rows = lax.broadcasted_iota(jnp.int32, x.shape, 0)
    vals = []
    for _ in range(count):
        m = jnp.max(x, axis=0, keepdims=True)
        vals.append(m)
        first = jnp.min(jnp.where(x == m, rows, x.shape[0]), axis=0, keepdims=True)
        x = jnp.where(rows == first, NEG_INF, x)
    return vals


def _rank_pairs():
    return [(m, n) for m in range(TOPK) for n in range(TOPK) if (m + 1) * (n + 1) <= TOPK]


def _peer_stats_kernel(q_ref, sk_ref, s1_ref, e1_ref, s2_ref, e2_ref, tau_ref, cand_ref):
    n_heads = s1_ref.shape[0]
    dk = sk_ref.shape[2]
    pairs = _rank_pairs()
    cand_ref[...] = jnp.full(cand_ref.shape, NEG_INF, F32)
    for h in range(n_heads):
        scores = []
        for c in range(2):
            qc = q_ref[:, (2 * h + c) * dk:(2 * h + c + 1) * dk].astype(BF16)
            scores.append(lax.dot_general(sk_ref[2 * h + c], qc, (((1,), (1,)), ((), ())),
                                          preferred_element_type=F32))
        top1 = _top_values(scores[0], TOPK)
        top2 = _top_values(scores[1], TOPK)
        for idx, (m, n) in enumerate(pairs):
            cand_ref[idx:idx + 1, :] = top1[m] + top2[n]
        best = _top_values(cand_ref[...], TOPK)
        z = sum(jnp.exp(b - best[0]) for b in best)
        e1 = jnp.exp(scores[0] - top1[0]) / z
        e2 = jnp.exp(scores[1] - top2[0])
        for c in range(s1_ref.shape[1]):
            cols = slice(c * LANES, (c + 1) * LANES)
            s1_ref[h, c] = scores[0][:, cols]
            s2_ref[h, c] = scores[1][:, cols]
            e1_ref[h, c] = e1[:, cols]
            e2_ref[h, c] = e2[:, cols]
            tau_ref[c, h:h + 1, :] = best[TOPK - 1][:, cols]


def _peer_stats(q, subkeys, tb):
    m = q.shape[0]
    n_hc, n_keys, dk = subkeys.shape
    n_heads = n_hc // 2
    assert m % tb == 0 and tb % LANES == 0
    cb = tb // LANES
    stat = pl.BlockSpec((n_heads, cb, n_keys, LANES), lambda i: (0, i, 0, 0))
    stat_shape = jax.ShapeDtypeStruct((n_heads, m // LANES, n_keys, LANES), F32)
    return pl.pallas_call(
        _peer_stats_kernel,
        grid=(m // tb,),
        in_specs=[pl.BlockSpec((tb, n_hc * dk), lambda i: (i, 0)),
                  pl.BlockSpec((n_hc, n_keys, dk), lambda i: (0, 0, 0))],
        out_specs=[stat, stat, stat, stat, pl.BlockSpec((cb, n_heads, LANES), lambda i: (i, 0, 0))],
        out_shape=[stat_shape] * 4 + [jax.ShapeDtypeStruct((m // LANES, n_heads, LANES), F32)],
        scratch_shapes=[pltpu.VMEM((-(-len(_rank_pairs()) // 8) * 8, tb), F32)],
        compiler_params=_params("parallel"),
        name="peer_stats",
    )(q, subkeys)


def _peer_mix_kernel(alpha, n_i, h_ref, u_ref, vt_ref, s1_ref, e1_ref, s2_ref, e2_ref, tau_ref,
                     g2_ref, b2_ref, y_ref, hb_ref, act_ref, coef_ref, acc_ref):
    j = pl.program_id(1)
    n_heads, n_c, n_keys, _ = s2_ref.shape

    @pl.when(j == 0)
    def _():
        hb_ref[...] = h_ref[...].astype(BF16)
        acc_ref[...] = jnp.zeros_like(acc_ref)

    act = lax.dot_general(u_ref[...], hb_ref[...], (((1,), (1,)), ((), ())), preferred_element_type=F32)
    for c in range(n_c):
        act_ref[c] = act[:, c * LANES:(c + 1) * LANES]

    def tile(idx, carry):
        il = idx // n_c
        c = idx % n_c
        rows = pl.ds(pl.multiple_of(il * n_keys, n_keys), n_keys)
        gate = jnp.zeros((n_keys, LANES), F32)
        for h in range(n_heads):
            s1 = s1_ref[h, c, pl.ds(il, 1), :]
            e1 = e1_ref[h, c, pl.ds(il, 1), :]
            tau = tau_ref[c, h:h + 1, :]
            gate = gate + jnp.where(s1 + s2_ref[h, c] >= tau, e2_ref[h, c], 0.0) * e1
        a = act_ref[c, rows, :]
        gelu = 0.5 * a * (1.0 + lax.erf(a * math.sqrt(0.5)))
        coef_ref[c, rows, :] = (gate * gelu).astype(BF16)
        return carry

    lax.fori_loop(0, n_i * n_c, tile, 0)
    coef = jnp.concatenate([coef_ref[c] for c in range(n_c)], axis=1)
    acc_ref[...] += jnp.dot(vt_ref[...], coef, preferred_element_type=F32)

    @pl.when(j == pl.num_programs(1) - 1)
    def _():
        y_ref[...] = _layer_norm(alpha * h_ref[...] + acc_ref[...].T, g2_ref[...], b2_ref[...])


def _peer_mix(h, u, vt, s1, e1, s2, e2, tau, ln2_g, ln2_b, alpha, tm, n_i):
    m, d = h.shape
    n_heads, _, n_keys, _ = s2.shape
    assert m % tm == 0 and tm % LANES == 0 and n_keys % n_i == 0 and n_i % 8 == 0
    tn = n_i * n_keys
    n_c = tm // LANES
    row_stat = pl.BlockSpec((n_heads, n_c, n_i, LANES), lambda i, j: (0, i, j, 0))
    once = pl.Buffered(1)
    full_stat = pl.BlockSpec((n_heads, n_c, n_keys, LANES), lambda i, j: (0, i, 0, 0), pipeline_mode=once)
    return pl.pallas_call(
        functools.partial(_peer_mix_kernel, alpha, n_i),
        grid=(m // tm, n_keys // n_i),
        in_specs=[pl.BlockSpec((tm, d), lambda i, j: (i, 0), pipeline_mode=once),
                  pl.BlockSpec((tn, d), lambda i, j: (j, 0)),
                  pl.BlockSpec((d, tn), lambda i, j: (0, j)),
                  row_stat, row_stat, full_stat, full_stat,
                  pl.BlockSpec((n_c, n_heads, LANES), lambda i, j: (i, 0, 0)),
                  pl.BlockSpec((1, d), lambda i, j: (0, 0)),
                  pl.BlockSpec((1, d), lambda i, j: (0, 0))],
        out_specs=pl.BlockSpec((tm, d), lambda i, j: (i, 0)),
        out_shape=jax.ShapeDtypeStruct((m, d), F32),
        scratch_shapes=[pltpu.VMEM((tm, d), BF16), pltpu.VMEM((n_c, tn, LANES), F32),
                        pltpu.VMEM((n_c, tn, LANES), BF16), pltpu.VMEM((d, tm), F32)],
        compiler_params=_params("parallel", "arbitrary"),
        name="peer_mix",
    )(h, u, vt, s1, e1, s2, e2, tau, ln2_g, ln2_b)


def _pick_tile(m, target):
    t = min(m, target)
    while m % t:
        t //= 2
    return t


def _layer_weights(w_in, mu_shift, w0, w_decay_up, a0, w_aaa_up, w_gate_up, k_k, k_a, r_k,
                   lnx_g, lnx_b, w_out, ln1_g, ln1_b, peer_wq, peer_subkeys, peer_u, peer_v,
                   ln2_g, ln2_b, sb_width):
    rw_width = w0.shape[0]
    d_lora, a_lora, g_lora = w_decay_up.shape[0], w_aaa_up.shape[0], w_gate_up.shape[0]
    in_cols = w_in.shape[1]
    rw_cols = in_cols - 3 * sb_width
    tail = rw_cols - 3 * rw_width
    assert tail == d_lora + a_lora + g_lora
    tail_w = -(-tail // LANES) * LANES
    cols_pad = 3 * sb_width + 3 * rw_width + tail_w
    assert sb_width % rw_width == 0 and (3 * sb_width + 3 * rw_width) % tail_w == 0
    row = lambda x: x.reshape(1, -1).astype(F32)
    mu = jnp.pad(mu_shift, (0, tail_w - tail))

    def lora(w, start):
        return jnp.zeros((tail_w, rw_width), F32).at[start:start + w.shape[0]].set(w).astype(BF16)

    n_ph, _, n_keys, dk = peer_subkeys.shape
    return dict(
        w_in=jnp.pad(w_in, ((0, 0), (0, cols_pad - in_cols))).astype(BF16),
        rw_prm=[row(mu[:rw_width]), row(mu[rw_width:2 * rw_width]), row(mu[2 * rw_width:3 * rw_width]),
                row(mu[3 * rw_width:]), row(w0), row(a0), row(k_k), row(k_a), row(r_k),
                lora(w_decay_up, 0), lora(w_aaa_up, d_lora), lora(w_gate_up, d_lora + a_lora),
                _head_ones()],
        lnx_g=row(lnx_g), lnx_b=row(lnx_b), w_out=w_out.astype(BF16),
        ln1_g=row(ln1_g), ln1_b=row(ln1_b), ln2_g=row(ln2_g), ln2_b=row(ln2_b),
        peer_wq=peer_wq.astype(BF16),
        subkeys=peer_subkeys.reshape(n_ph * 2, n_keys, dk).astype(BF16),
        peer_u=peer_u.astype(BF16), peer_vt=peer_v.astype(BF16).T,
        rw_width=rw_width, rw_cols=rw_cols, tail_w=tail_w, cols_pad=cols_pad,
    )


def _in_proj(x, lw):
    m = x.shape[0]
    n = lw["cols_pad"]
    tn = n // 3 if n % (3 * LANES) == 0 else n
    return _matmul(x, lw["w_in"], _pick_tile(m, 512), tn, "in_proj")


def _peer(h, lw, alpha):
    m = h.shape[0]
    m_pad = -(-m // LANES) * LANES
    hp = jnp.pad(h, ((0, m_pad - m), (0, 0))) if m_pad != m else h
    q = _matmul(hp, lw["peer_wq"], _pick_tile(m_pad, 512), lw["peer_wq"].shape[1], "peer_q")
    s1, e1, s2, e2, tau = _peer_stats(q, lw["subkeys"], _pick_tile(m_pad, 256))
    y = _peer_mix(hp, lw["peer_u"], lw["peer_vt"], s1, e1, s2, e2, tau, lw["ln2_g"], lw["ln2_b"],
                  alpha, _pick_tile(m_pad, 512), 8)
    return y[:m]


def _wkv_to_lanes(s):
    b, h, v, k = s.shape
    return s.transpose(0, 2, 1, 3).reshape(b, v, h * k)


def _wkv_from_lanes(s, n_heads):
    b, v, hk = s.shape
    return s.reshape(b, v, n_heads, hk // n_heads).transpose(0, 2, 1, 3)


def kernel(x_prompt, x_sample, cache_sb_k, cache_sb_v, state_wkv, state_shift, page_table, w_in, sb_bias, mu_shift, w0, w_decay_up, a0, w_aaa_up, w_gate_up, k_k, k_a, r_k, lnx_g, lnx_b, w_out, ln1_g, ln1_b, peer_wq, peer_subkeys, peer_u, peer_v, ln2_g, ln2_b):
    depth = w_in.shape[0]
    bsz, seq, d_model = x_prompt.shape
    dec_b, dec_seq, _ = x_sample.shape
    assert dec_seq == 1, "one new token per sampled sequence"
    _, n_pool, page, sb_heads, head_dim = cache_sb_k.shape
    rw_heads = state_wkv.shape[2]
    assert head_dim == HEAD_DIM and state_wkv.shape[3] == HEAD_DIM
    sb_width = sb_heads * HEAD_DIM
    alpha = (2 * depth) ** 0.25

    h_p = [x_prompt[b] for b in range(bsz)]
    h_s = x_sample[:, 0]
    outs = {name: [] for name in ("kp", "vp", "wp", "sp", "ks", "vs", "ws", "ss")}
    for l in range(depth):
        lw = _layer_weights(w_in[l], mu_shift[l], w0[l], w_decay_up[l], a0[l], w_aaa_up[l], w_gate_up[l],
                            k_k[l], k_a[l], r_k[l].reshape(-1), lnx_g[l], lnx_b[l], w_out[l], ln1_g[l],
                            ln1_b[l], peer_wq[l], peer_subkeys[l], peer_u[l], peer_v[l], ln2_g[l],
                            ln2_b[l], sb_width)
        rw_width, rw_cols, tail_w = lw["rw_width"], lw["rw_cols"], lw["tail_w"]
        rw0 = 3 * sb_width

        p_s = _in_proj(h_s, lw)
        pool_t = lambda c: c.transpose(0, 2, 3, 1).reshape(n_pool, sb_width, page)
        n_pp = _pick_tile(page_table.shape[1], 8)
        tok_args = (p_s[:, :sb_width], sb_bias[l], pool_t(cache_sb_k[l]), pool_t(cache_sb_v[l]), page_table, n_pp)
        tok_steps = dec_b * (page_table.shape[1] // n_pp)
        sb_s = None

        kp, vp, wp, sp = [], [], [], []
        for b in range(bsz):
            x = h_p[b]
            p = _in_proj(x, lw)
            kp.append(p[:, sb_width:2 * sb_width].reshape(seq, sb_heads, HEAD_DIM))
            vp.append(p[:, 2 * sb_width:3 * sb_width].reshape(seq, sb_heads, HEAD_DIM))
            sp.append(p[seq - 1:seq, rw0:rw0 + rw_cols])
            zeros = [jnp.zeros((1, rw_width), F32)] * 3 + [jnp.zeros((1, tail_w), F32)]
            r, w, k, v, a, bb, gate, bonus = _rw_prep(p, None, zeros, lw["rw_prm"], _pick_tile(seq, 256),
                                                      rw_width, rw0, tail_w)
            seqs = [u[None] for u in (r, w, k, v, a, bb)]
            s0 = jnp.zeros((1, HEAD_DIM, rw_width), F32)
            if sb_s is None and seq % tok_steps == 0 and (seq // tok_steps) % 8 == 0:
                sb_s, y, s_fin = _sb_tok(*tok_args, scan=(seqs, s0))
            else:
                y, s_fin = _rw_scan(seqs, s0, _pick_tile(seq, 128))
            wp.append(_wkv_from_lanes(s_fin, rw_heads)[0])
            tq = _pick_tile(seq, 512)
            sb = _sb_seq(p, sb_bias[l], sb_heads, tq, _pick_tile(tq, 256))
            h = _mix(y[0], bonus, gate, sb, x, lw["w_out"], lw["lnx_g"], lw["lnx_b"], lw["ln1_g"],
                     lw["ln1_b"], alpha, _pick_tile(seq, 256))
            h_p[b] = _peer(h, lw, alpha)
        outs["kp"].append(jnp.stack(kp)); outs["vp"].append(jnp.stack(vp))
        outs["wp"].append(jnp.stack(wp)); outs["sp"].append(jnp.stack(sp))

        p = p_s
        outs["ks"].append(p[:, sb_width:2 * sb_width].reshape(dec_b, 1, sb_heads, HEAD_DIM))
        outs["vs"].append(p[:, 2 * sb_width:3 * sb_width].reshape(dec_b, 1, sb_heads, HEAD_DIM))
        outs["ss"].append(p[:, None, rw0:rw0 + rw_cols])
        shift = state_shift[l][:, 0]
        prev = [shift[:, :rw_width], shift[:, rw_width:2 * rw_width], shift[:, 2 * rw_width:3 * rw_width],
                jnp.pad(shift[:, 3 * rw_width:], ((0, 0), (0, tail_w - (rw_cols - 3 * rw_width))))]
        r, w, k, v, a, bb, gate, bonus = _rw_prep(p, prev, None, lw["rw_prm"], _pick_tile(dec_b, 256),
                                                  rw_width, rw0, tail_w)
        y, s_fin = _rw_scan([u[:, None] for u in (r, w, k, v, a, bb)], _wkv_to_lanes(state_wkv[l]), 1)
        outs["ws"].append(_wkv_from_lanes(s_fin, rw_heads))
        if sb_s is None:
            sb_s = _sb_tok(*tok_args)
        h = _mix(y[:, 0], bonus, gate, sb_s.reshape(dec_b, sb_width), h_s, lw["w_out"], lw["lnx_g"], lw["lnx_b"],
                 lw["ln1_g"], lw["ln1_b"], alpha, _pick_tile(dec_b, 256))
        h_s = _peer(h, lw, alpha)

    stack = lambda name: jnp.stack(outs[name])
    return (jnp.stack(h_p), h_s[:, None], stack("kp"), stack("vp"), stack("wp"), stack("sp"),
            stack("ks"), stack("vs"), stack("ws"), stack("ss"))
```

```python
import functools
import math

import numpy as np
import jax
import jax.numpy as jnp
from jax import lax
from jax.experimental import pallas as pl
from jax.experimental.pallas import tpu as pltpu

F32 = jnp.float32
BF16 = jnp.bfloat16

LANES = 128
HEAD_DIM = 64
SEG = 256
GN_EPS = 64e-5
LN_EPS = 1e-5
TOPK = 16
NEG_INF = float("-inf")
VMEM_LIMIT_BYTES = 56 * 1024 * 1024


def _params(*semantics):
    return pltpu.CompilerParams(dimension_semantics=semantics, vmem_limit_bytes=VMEM_LIMIT_BYTES)


def _head_ones():
    blocks = SEG // HEAD_DIM
    return jnp.asarray(np.kron(np.eye(blocks), np.ones((HEAD_DIM, HEAD_DIM))), dtype=BF16)


def _split_bf16(x):
    hi = x.astype(BF16)
    lo = (x - hi.astype(F32)).astype(BF16)
    return hi, lo


def _head_sum(x, ones):
    outs = []
    for g in range(x.shape[-1] // SEG):
        hi, lo = _split_bf16(x[:, g * SEG:(g + 1) * SEG])
        outs.append(jnp.dot(hi, ones, preferred_element_type=F32)
                    + jnp.dot(lo, ones, preferred_element_type=F32))
    return outs[0] if len(outs) == 1 else jnp.concatenate(outs, axis=-1)


def _softplus(x):
    return jnp.maximum(x, 0.0) + jnp.log1p(jnp.exp(-jnp.abs(x)))


def _softplus_pos(x):
    return jnp.maximum(x, 0.0) + jnp.log(1.0 + jnp.exp(-jnp.abs(x)))


def _layer_norm(x, g, b):
    mu = jnp.mean(x, axis=-1, keepdims=True)
    xc = x - mu
    var = jnp.mean(xc * xc, axis=-1, keepdims=True)
    return xc * lax.rsqrt(var + LN_EPS) * g + b


def _matmul_kernel(x_ref, w_ref, o_ref):
    o_ref[...] = jnp.dot(x_ref[...].astype(BF16), w_ref[...], preferred_element_type=F32)


def _matmul(x, w, tm, tn, name):
    m, k = x.shape
    n = w.shape[1]
    assert m % tm == 0 and n % tn == 0
    return pl.pallas_call(
        _matmul_kernel,
        grid=(n // tn, m // tm),
        in_specs=[pl.BlockSpec((tm, k), lambda j, i: (i, 0)),
                  pl.BlockSpec((k, tn), lambda j, i: (0, j))],
        out_specs=pl.BlockSpec((tm, tn), lambda j, i: (i, j)),
        out_shape=jax.ShapeDtypeStruct((m, n), F32),
        compiler_params=_params("parallel", "parallel"),
        name=name,
    )(x, w)


def _rw_prep_kernel(halo, *refs):
    if halo:
        (pr, pk, pv, pt, hr, hk, hv, ht, sr, sk, sv, st,
         mur, muk, muv, mut, w0, a0, kk_s, ka_s, rk_s, wd, wa, wg, ones_ref,
         r_o, w_o, k_o, v_o, a_o, b_o, g_o, bon_o) = refs
    else:
        (pr, pk, pv, pt, qr, qk, qv, qt,
         mur, muk, muv, mut, w0, a0, kk_s, ka_s, rk_s, wd, wa, wg, ones_ref,
         r_o, w_o, k_o, v_o, a_o, b_o, g_o, bon_o) = refs

    def shifted(x_ref, idx):
        x = x_ref[...]
        if not halo:
            return x, (qr, qk, qv, qt)[idx][...]
        h_ref = (hr, hk, hv, ht)[idx]
        s_ref = (sr, sk, sv, st)[idx]
        first = jnp.where(pl.program_id(0) == 0, s_ref[...], h_ref[7:8, :])
        row = lax.broadcasted_iota(jnp.int32, x.shape, 0)
        return x, jnp.where(row == 0, first, pltpu.roll(x, 1, axis=0))

    def lerp(x_ref, idx, mu_ref):
        x, prev = shifted(x_ref, idx)
        return x + (prev - x) * mu_ref[...]

    ones = ones_ref[...]
    xr = lerp(pr, 0, mur)
    xk = lerp(pk, 1, muk)
    xv = lerp(pv, 2, muv)
    xt = lerp(pt, 3, mut)

    dec_in = jnp.dot(jnp.tanh(xt).astype(BF16), wd[...], preferred_element_type=F32)
    w = -_softplus(-(w0[...] + dec_in)) - 0.5
    decay = jnp.exp(-jnp.exp(w))
    iclr = jax.nn.sigmoid(a0[...] + jnp.dot(xt.astype(BF16), wa[...], preferred_element_type=F32))
    gate = jnp.dot(jax.nn.sigmoid(xt).astype(BF16), wg[...], preferred_element_type=F32)

    kk = xk * kk_s[...]
    norm = jnp.sqrt(_head_sum(kk * kk, ones))
    kk = kk / jnp.maximum(norm, 1e-12)
    k2 = xk * (1.0 + (iclr - 1.0) * ka_s[...])
    bonus = _head_sum(xr * k2 * rk_s[...], ones) * xv

    r_o[...] = xr
    w_o[...] = decay
    k_o[...] = k2
    v_o[...] = xv
    a_o[...] = -kk
    b_o[...] = kk * iclr
    g_o[...] = gate
    bon_o[...] = bonus


def _rw_prep(p, prev, shift_rows, prm, tm, rw_width, sb_cols, tail_w):
    m = p.shape[0]
    assert m % tm == 0
    cb = sb_cols // rw_width
    tb = (sb_cols + 3 * rw_width) // tail_w
    assert cb * rw_width == sb_cols and tb * tail_w == sb_cols + 3 * rw_width
    halo = prev is None

    def col(width, c):
        return pl.BlockSpec((tm, width), lambda i, c=c: (i, c))

    def halo_col(width, c):
        return pl.BlockSpec((8, width), lambda i, c=c: (jnp.maximum(i * (tm // 8) - 1, 0), c))

    def row(width):
        return pl.BlockSpec((1, width), lambda i: (0, 0))

    def whole(shape):
        return pl.BlockSpec(shape, lambda i: (0,) * len(shape))

    main_specs = [col(rw_width, cb), col(rw_width, cb + 1), col(rw_width, cb + 2), col(tail_w, tb)]
    if halo:
        assert tm % 8 == 0
        extra_specs = [halo_col(rw_width, cb), halo_col(rw_width, cb + 1), halo_col(rw_width, cb + 2),
                       halo_col(tail_w, tb), row(rw_width), row(rw_width), row(rw_width), row(tail_w)]
        extra_args = [p, p, p, p] + list(shift_rows)
    else:
        extra_specs = [pl.BlockSpec((tm, rw_width), lambda i: (i, 0))] * 3 + [pl.BlockSpec((tm, tail_w), lambda i: (i, 0))]
        extra_args = list(prev)
    prm_specs = [row(rw_width), row(rw_width), row(rw_width), row(tail_w),
                 row(rw_width), row(rw_width), row(rw_width), row(rw_width), row(rw_width),
                 whole((tail_w, rw_width)), whole((tail_w, rw_width)), whole((tail_w, rw_width)),
                 whole((SEG, SEG))]
    out_spec = pl.BlockSpec((tm, rw_width), lambda i: (i, 0))
    out_shape = jax.ShapeDtypeStruct((m, rw_width), F32)
    return pl.pallas_call(
        functools.partial(_rw_prep_kernel, halo),
        grid=(m // tm,),
        in_specs=main_specs + extra_specs + prm_specs,
        out_specs=[out_spec] * 8,
        out_shape=[out_shape] * 8,
        compiler_params=_params("parallel"),
        name="rw_prep_seq" if halo else "rw_prep_tok",
    )(p, p, p, p, *extra_args, *prm)


def _rw_scan_kernel(*refs):
    chunk = pl.program_id(1)
    _rw_scan_body(chunk == 0, chunk == pl.num_programs(1) - 1, *refs)


def _rw_scan_body(first, last, r_ref, w_ref, k_ref, v_ref, a_ref, b_ref, s0_ref, ones_ref,
                  y_ref, sT_ref, s_ref, vb_ref):
    tc = r_ref.shape[0]
    width = r_ref.shape[1]
    groups = width // SEG

    @pl.when(first)
    def _():
        for g in range(groups):
            s_ref[g] = s0_ref[:, g * SEG:(g + 1) * SEG]

    ones = ones_ref[...]
    lane = lax.broadcasted_iota(jnp.int32, (HEAD_DIM, SEG), 1)
    sub = lax.broadcasted_iota(jnp.int32, (HEAD_DIM, SEG), 0)
    diag = (lane % HEAD_DIM) == sub

    sls = [slice(g * SEG, (g + 1) * SEG) for g in range(groups)]
    n = groups * HEAD_DIM

    diag_next = (lane % HEAD_DIM) == ((sub + 1) % HEAD_DIM)
    first_lane = (lax.broadcasted_iota(jnp.int32, (1, width), 1) % HEAD_DIM) == 0

    def spread_rows(t):
        v = v_ref[pl.ds(t, 1), :]
        v_hi = v.astype(BF16).astype(F32)
        v_lo = v - v_hi
        v_lo = jnp.where(first_lane, pltpu.roll(v_lo, width - (HEAD_DIM - 1), axis=1), pltpu.roll(v_lo, 1, axis=1))
        return [jnp.where(diag, v_hi[:, sl], jnp.where(diag_next, v_lo[:, sl], 0.0)).astype(BF16) for sl in sls]

    def put_vb(res):
        for g in range(groups):
            vb_ref[g] = res[g * HEAD_DIM:(g + 1) * HEAD_DIM]

    put_vb(jnp.dot(jnp.concatenate(spread_rows(0), axis=0), ones, preferred_element_type=F32))

    def step(t, carry):
        r = r_ref[pl.ds(t, 1), :]
        w = w_ref[pl.ds(t, 1), :]
        k = k_ref[pl.ds(t, 1), :]
        a = a_ref[pl.ds(t, 1), :]
        b = b_ref[pl.ds(t, 1), :]
        sa = jnp.dot(jnp.concatenate([(s_ref[g] * a[:, sls[g]]).astype(BF16) for g in range(groups)], axis=0),
                     ones, preferred_element_type=F32)
        outs = []
        for g in range(groups):
            sl = sls[g]
            rows = slice(g * HEAD_DIM, (g + 1) * HEAD_DIM)
            s_new = s_ref[g] * w[:, sl] + sa[rows] * b[:, sl] + vb_ref[g] * k[:, sl]
            s_ref[g] = s_new
            outs.append((s_new * r[:, sl]).astype(BF16))
        res = jnp.dot(jnp.concatenate(outs + spread_rows(jnp.minimum(t + 1, tc - 1)), axis=0), ones,
                      preferred_element_type=F32)
        ys = [jnp.sum(jnp.where(diag, res[g * HEAD_DIM:(g + 1) * HEAD_DIM], 0.0), axis=0, keepdims=True)
              for g in range(groups)]
        y_ref[pl.ds(t, 1), :] = jnp.concatenate(ys, axis=-1)
        put_vb(res[n:2 * n])
        return carry

    lax.fori_loop(0, tc, step, 0, unroll=tc if tc <= 32 else math.gcd(tc, 8))

    @pl.when(last)
    def _():
        for g in range(groups):
            sT_ref[:, g * SEG:(g + 1) * SEG] = s_ref[g]


def _rw_scan(seqs, s0, tc):
    bsz, t, width = seqs[0].shape
    assert t % tc == 0 and width % SEG == 0
    seq_spec = pl.BlockSpec((None, tc, width), lambda b, c: (b, c, 0))
    st_spec = pl.BlockSpec((None, HEAD_DIM, width), lambda b, c: (b, 0, 0))
    return pl.pallas_call(
        _rw_scan_kernel,
        grid=(bsz, t // tc),
        in_specs=[seq_spec] * 6 + [st_spec, pl.BlockSpec((SEG, SEG), lambda b, c: (0, 0))],
        out_specs=[seq_spec, st_spec],
        out_shape=[jax.ShapeDtypeStruct((bsz, t, width), F32),
                   jax.ShapeDtypeStruct((bsz, HEAD_DIM, width), F32)],
        scratch_shapes=[pltpu.VMEM((width // SEG, HEAD_DIM, SEG), F32)] * 2,
        compiler_params=_params("parallel", "arbitrary"),
        name="rw_scan",
    )(*seqs, s0, _head_ones())


def _sb_seq_kernel(scale, tk, bias_ref, q_ref, k_ref, v_ref, tri_ref, o_ref, qs_ref, acc_ref, run_ref):
    tq = q_ref.shape[0]
    per_q = tq // tk
    hp = pl.program_id(0)
    qi = pl.program_id(1)
    lane = lax.broadcasted_iota(jnp.int32, (tq, LANES), 1)
    q = q_ref[...] * scale
    qs_ref[0] = jnp.where(lane < HEAD_DIM, q, 0.0).astype(BF16)
    qs_ref[1] = jnp.where(lane >= HEAD_DIM, q, 0.0).astype(BF16)
    acc_ref[...] = jnp.zeros_like(acc_ref)
    run_ref[...] = jnp.zeros_like(run_ref)
    bias = [bias_ref[2 * hp], bias_ref[2 * hp + 1]]

    def block(kb, masked):
        start = pl.multiple_of(kb * tk, tk)
        kblk = k_ref[pl.ds(start, tk), :].astype(BF16)
        vblk = v_ref[pl.ds(start, tk), :].astype(BF16)
        tri = tri_ref[...]
        if masked:
            row = lax.broadcasted_iota(jnp.int32, (tq, tk), 0) + qi * tq
            colm = lax.broadcasted_iota(jnp.int32, (tq, tk), 1) + start
            causal = colm < row
        for s in range(2):
            z = lax.dot_general(qs_ref[s], kblk, (((1,), (1,)), ((), ())),
                                preferred_element_type=F32) + bias[s]
            drop = _softplus_pos(z)
            if masked:
                drop = jnp.where(causal, drop, 0.0)
            later = jnp.dot(drop.astype(BF16), tri, preferred_element_type=F32)
            att = jnp.exp(z - drop + later)
            if masked:
                att = jnp.where(causal, att, 0.0)
            pv = jnp.dot(att.astype(BF16), vblk, preferred_element_type=F32)
            acc_ref[s] += jnp.exp(run_ref[s]) * pv
            run_ref[s] += jnp.broadcast_to(later[:, 0:1] - drop[:, 0:1], (tq, LANES))

    for d in range(per_q):
        block(qi * per_q + per_q - 1 - d, True)

    def body(i, carry):
        for d in range(per_q):
            block((qi - 1 - i) * per_q + per_q - 1 - d, False)
        return carry

    lax.fori_loop(0, qi, body, 0)
    o_ref[...] = jnp.where(lane < HEAD_DIM, acc_ref[0], acc_ref[1])


def _sb_tri(tk):
    j = np.arange(tk)[:, None]
    s = np.arange(tk)[None, :]
    return jnp.asarray(-(j > s).astype(np.float32), dtype=BF16)


def _sb_seq(p, bias, n_heads, tq, tk):
    t = p.shape[0]
    assert t % tq == 0 and tq % tk == 0 and n_heads % 2 == 0
    pairs = n_heads // 2
    scale = 1.0 / math.sqrt(HEAD_DIM)
    return pl.pallas_call(
        functools.partial(_sb_seq_kernel, scale, tk),
        grid=(pairs, t // tq),
        in_specs=[pl.BlockSpec(memory_space=pltpu.SMEM),
                  pl.BlockSpec((tq, LANES), lambda h, i: (i, h)),
                  pl.BlockSpec((t, LANES), lambda h, i: (0, pairs + h)),
                  pl.BlockSpec((t, LANES), lambda h, i: (0, 2 * pairs + h)),
                  pl.BlockSpec((tk, tk), lambda h, i: (0, 0))],
        out_specs=pl.BlockSpec((tq, LANES), lambda h, i: (i, h)),
        out_shape=jax.ShapeDtypeStruct((t, n_heads * HEAD_DIM), F32),
        scratch_shapes=[pltpu.VMEM((2, tq, LANES), BF16), pltpu.VMEM((2, tq, LANES), F32),
                        pltpu.VMEM((2, tq, LANES), F32)],
        compiler_params=_params("parallel", "arbitrary"),
        name="sb_seq",
    )(bias, p, p, p, _sb_tri(tk))


def _sb_tok_kernel(n_pp, scale, pt_ref, *refs):
    j = pl.program_id(1)
    _sb_tok_body(n_pp, scale, j == 0, j == pl.num_programs(1) - 1, *refs)


def _sb_tok_body(n_pp, scale, first, last, bias_ref, q_ref, *refs):
    k_refs = refs[:n_pp]
    v_refs = refs[n_pp:2 * n_pp]
    tri_ref, o_ref, acc_ref, run_ref = refs[2 * n_pp:]
    width, page = acc_ref.shape
    n_heads = width // HEAD_DIM

    @pl.when(first)
    def _():
        acc_ref[...] = jnp.zeros_like(acc_ref)
        run_ref[...] = jnp.zeros_like(run_ref)

    q = q_ref[...]
    tri = tri_ref[...]
    bias = bias_ref[...]
    for i in range(n_pp):
        prod = k_refs[i][...] * q
        z = jnp.sum(prod.reshape(n_heads, HEAD_DIM, page), axis=1) * scale + bias
        drop = _softplus_pos(z)
        hi, lo = _split_bf16(-drop)
        sums = jnp.dot(jnp.concatenate([hi, lo], axis=1), tri, preferred_element_type=F32)
        att = jnp.exp(z - drop + sums[:, :page] + run_ref[...])
        att_wide = jnp.broadcast_to(att[:, None, :], (n_heads, HEAD_DIM, page)).reshape(width, page)
        acc_ref[...] += att_wide * v_refs[i][...]
        run_ref[...] += sums[:, page:]

    @pl.when(last)
    def _():
        o_ref[...] = jnp.sum(acc_ref[...], axis=1, keepdims=True)


def _scan_tok_kernel(n_pp, scale, steps, n_tok_in, pt_ref, *refs):
    s = pl.program_id(0)
    j = s % steps
    tok_in = refs[:n_tok_in]
    scan_in = refs[n_tok_in:n_tok_in + 8]
    o_tok, y_ref, sT_ref, acc_ref, run_ref, s_ref, vb_ref = refs[n_tok_in + 8:]
    _rw_scan_body(s == 0, s == pl.num_programs(0) - 1, *scan_in, y_ref, sT_ref, s_ref, vb_ref)
    _sb_tok_body(n_pp, scale, j == 0, j == steps - 1, *tok_in, o_tok, acc_ref, run_ref)


def _sb_tok(q, bias, cache_k, cache_v, page_table, n_pp, scan=None):
    bsz, width = q.shape
    _, _, page = cache_k.shape
    n_pages = page_table.shape[1]
    n_heads = width // HEAD_DIM
    assert n_pages % n_pp == 0
    steps = n_pages // n_pp
    scale = 1.0 / math.sqrt(HEAD_DIM)
    jj = np.arange(page)[:, None]
    ss = np.arange(page)[None, :]
    half = np.concatenate([(jj > ss), np.ones((page, page), bool)], axis=1)
    tri = jnp.asarray(np.concatenate([half, half], axis=0), dtype=BF16)
    bias_rows = jnp.broadcast_to(bias.astype(F32)[:, None], (n_heads, page))
    q_cols = jnp.broadcast_to(q[:, :, None], (bsz, width, page))
    tok_args = [bias_rows, q_cols] + [cache_k] * n_pp + [cache_v] * n_pp + [tri]
    tok_scratch = [pltpu.VMEM((width, page), F32), pltpu.VMEM((n_heads, page), F32)]
    tok_shape = jax.ShapeDtypeStruct((bsz, width, 1), F32)

    def tok_specs(split):
        def page_spec(i):
            def index(*idx):
                b, j = split(*idx[:-1])
                return (idx[-1][b, n_pages - 1 - (j * n_pp + i)], 0, 0)
            return pl.BlockSpec((None, width, page), index)

        def const(shape):
            return pl.BlockSpec(shape, lambda *idx: (0,) * len(shape))

        per_seq = lambda last_dim: pl.BlockSpec((None, width, last_dim), lambda *idx: (split(*idx[:-1])[0], 0, 0))
        ins = ([const((n_heads, page)), per_seq(page)] + [page_spec(i) for i in range(n_pp)] * 2
               + [const((2 * page, 2 * page))])
        return ins, per_seq(1)

    if scan is None:
        ins, out = tok_specs(lambda b, j: (b, j))
        return pl.pallas_call(
            functools.partial(_sb_tok_kernel, n_pp, scale),
            grid_spec=pltpu.PrefetchScalarGridSpec(num_scalar_prefetch=1, grid=(bsz, steps), in_specs=ins,
                                                   out_specs=out, scratch_shapes=tok_scratch),
            out_shape=tok_shape,
            compiler_params=_params("parallel", "arbitrary"),
            name="sb_tok",
        )(page_table, *tok_args)

    seqs, s0 = scan
    _, t, rw_width = seqs[0].shape
    n_steps = bsz * steps
    tc = t // n_steps
    assert tc * n_steps == t and tc % 8 == 0 and seqs[0].shape[0] == 1
    ins, out = tok_specs(lambda s: (s // steps, s % steps))
    seq_spec = pl.BlockSpec((None, tc, rw_width), lambda s, pt: (0, s, 0))
    st_spec = pl.BlockSpec((None, HEAD_DIM, rw_width), lambda s, pt: (0, 0, 0))
    state = pltpu.VMEM((rw_width // SEG, HEAD_DIM, SEG), F32)
    return pl.pallas_call(
        functools.partial(_scan_tok_kernel, n_pp, scale, steps, len(ins)),
        grid_spec=pltpu.PrefetchScalarGridSpec(
            num_scalar_prefetch=1, grid=(n_steps,),
            in_specs=ins + [seq_spec] * 6 + [st_spec, pl.BlockSpec((SEG, SEG), lambda s, pt: (0, 0))],
            out_specs=[out, seq_spec, st_spec],
            scratch_shapes=tok_scratch + [state, state]),
        out_shape=[tok_shape, jax.ShapeDtypeStruct((1, t, rw_width), F32),
                   jax.ShapeDtypeStruct((1, HEAD_DIM, rw_width), F32)],
        compiler_params=_params("arbitrary"),
        name="scan_tok",
    )(page_table, *tok_args, *seqs, s0, _head_ones())


def _mix_kernel(alpha, y_ref, bon_ref, gate_ref, sb_ref, x_ref, wo_ref, lg_ref, lb_ref,
                g1_ref, b1_ref, ones_ref, h_ref):
    ones = ones_ref[...]
    rw_width = y_ref.shape[1]
    y = y_ref[...]
    inv_n = 1.0 / HEAD_DIM
    mean = _head_sum(y, ones) * inv_n
    yc = y - mean
    var = _head_sum(yc * yc, ones) * inv_n
    yn = yc * lax.rsqrt(var + GN_EPS) * lg_ref[...] + lb_ref[...]
    rw = (yn + bon_ref[...]) * gate_ref[...]
    mix = (jnp.dot(rw.astype(BF16), wo_ref[0:rw_width, :], preferred_element_type=F32)
           + jnp.dot(sb_ref[...].astype(BF16), wo_ref[rw_width:, :], preferred_element_type=F32))
    h_ref[...] = _layer_norm(alpha * x_ref[...] + mix, g1_ref[...], b1_ref[...])


def _mix(y, bonus, gate, sb, x, wo, lnx_g, lnx_b, ln1_g, ln1_b, alpha, tm):
    m, d = x.shape
    rw_width, sb_width = y.shape[1], sb.shape[1]
    assert m % tm == 0

    def tile(width):
        return pl.BlockSpec((tm, width), lambda i: (i, 0))

    def whole(shape):
        return pl.BlockSpec(shape, lambda i: (0,) * len(shape))

    return pl.pallas_call(
        functools.partial(_mix_kernel, alpha),
        grid=(m // tm,),
        in_specs=[tile(rw_width), tile(rw_width), tile(rw_width), tile(sb_width), tile(d),
                  whole(wo.shape), whole((1, rw_width)), whole((1, rw_width)),
                  whole((1, d)), whole((1, d)), whole((SEG, SEG))],
        out_specs=tile(d),
        out_shape=jax.ShapeDtypeStruct((m, d), F32),
        compiler_params=_params("parallel"),
        name="mix_ln",
    )(y, bonus, gate, sb, x, wo, lnx_g, lnx_b, ln1_g, ln1_b, _head_ones())


def _top_values(x, count):
    rows = lax.broadcasted_iota(jnp.int32, x.shape, 0)
    vals = []
    for _ in range(count):
        m = jnp.max(x, axis=0, keepdims=True)
        vals.append(m)
        first = jnp.min(jnp.where(x == m, rows, x.shape[0]), axis=0, keepdims=True)
        x = jnp.where(rows == first, NEG_INF, x)
    return vals


def _rank_pairs():
    return [(m, n) for m in range(TOPK) for n in range(TOPK) if (m + 1) * (n + 1) <= TOPK]


def _peer_stats_kernel(q_ref, sk_ref, s1_ref, e1_ref, s2_ref, e2_ref, tau_ref, cand_ref):
    n_heads = s1_ref.shape[0]
    dk = sk_ref.shape[2]
    pairs = _rank_pairs()
    cand_ref[...] = jnp.full(cand_ref.shape, NEG_INF, F32)
    for h in range(n_heads):
        scores = []
        for c in range(2):
            qc = q_ref[:, (2 * h + c) * dk:(2 * h + c + 1) * dk].astype(BF16)
            scores.append(lax.dot_general(sk_ref[2 * h + c], qc, (((1,), (1,)), ((), ())),
                                          preferred_element_type=F32))
        top1 = _top_values(scores[0], TOPK)
        top2 = _top_values(scores[1], TOPK)
        for idx, (m, n) in enumerate(pairs):
            cand_ref[idx:idx + 1, :] = top1[m] + top2[n]
        best = _top_values(cand_ref[...], TOPK)
        z = sum(jnp.exp(b - best[0]) for b in best)
        e1 = jnp.exp(scores[0] - top1[0]) / z
        e2 = jnp.exp(scores[1] - top2[0])
        for c in range(s1_ref.shape[1]):
            cols = slice(c * LANES, (c + 1) * LANES)
            s1_ref[h, c] = scores[0][:, cols]
            s2_ref[h, c] = scores[1][:, cols]
            e1_ref[h, c] = e1[:, cols]
            e2_ref[h, c] = e2[:, cols]
            tau_ref[c, h:h + 1, :] = best[TOPK - 1][:, cols]


def _peer_stats(q, subkeys, tb):
    m = q.shape[0]
    n_hc, n_keys, dk = subkeys.shape
    n_heads = n_hc // 2
    assert m % tb == 0 and tb % LANES == 0
    cb = tb // LANES
    stat = pl.BlockSpec((n_heads, cb, n_keys, LANES), lambda i: (0, i, 0, 0))
    stat_shape = jax.ShapeDtypeStruct((n_heads, m // LANES, n_keys, LANES), F32)
    return pl.pallas_call(
        _peer_stats_kernel,
        grid=(m // tb,),
        in_specs=[pl.BlockSpec((tb, n_hc * dk), lambda i: (i, 0)),
                  pl.BlockSpec((n_hc, n_keys, dk), lambda i: (0, 0, 0))],
        out_specs=[stat, stat, stat, stat, pl.BlockSpec((cb, n_heads, LANES), lambda i: (i, 0, 0))],
        out_shape=[stat_shape] * 4 + [jax.ShapeDtypeStruct((m // LANES, n_heads, LANES), F32)],
        scratch_shapes=[pltpu.VMEM((-(-len(_rank_pairs()) // 8) * 8, tb), F32)],
        compiler_params=_params("parallel"),
        name="peer_stats",
    )(q, subkeys)


def _peer_mix_kernel(alpha, n_i, h_ref, u_ref, vt_ref, s1_ref, e1_ref, s2_ref, e2_ref, tau_ref,
                     g2_ref, b2_ref, y_ref, hb_ref, act_ref, coef_ref, acc_ref):
    j = pl.program_id(1)
    n_heads, n_c, n_keys, _ = s2_ref.shape

    @pl.when(j == 0)
    def _():
        hb_ref[...] = h_ref[...].astype(BF16)
        acc_ref[...] = jnp.zeros_like(acc_ref)

    act = lax.dot_general(u_ref[...], hb_ref[...], (((1,), (1,)), ((), ())), preferred_element_type=F32)
    for c in range(n_c):
        act_ref[c] = act[:, c * LANES:(c + 1) * LANES]

    def tile(idx, carry):
        il = idx // n_c
        c = idx % n_c
        rows = pl.ds(pl.multiple_of(il * n_keys, n_keys), n_keys)
        gate = jnp.zeros((n_keys, LANES), F32)
        for h in range(n_heads):
            s1 = s1_ref[h, c, pl.ds(il, 1), :]
            e1 = e1_ref[h, c, pl.ds(il, 1), :]
            tau = tau_ref[c, h:h + 1, :]
            gate = gate + jnp.where(s1 + s2_ref[h, c] >= tau, e2_ref[h, c], 0.0) * e1
        a = act_ref[c, rows, :]
        gelu = 0.5 * a * (1.0 + lax.erf(a * math.sqrt(0.5)))
        coef_ref[c, rows, :] = (gate * gelu).astype(BF16)
        return carry

    lax.fori_loop(0, n_i * n_c, tile, 0)
    coef = jnp.concatenate([coef_ref[c] for c in range(n_c)], axis=1)
    acc_ref[...] += jnp.dot(vt_ref[...], coef, preferred_element_type=F32)

    @pl.when(j == pl.num_programs(1) - 1)
    def _():
        y_ref[...] = _layer_norm(alpha * h_ref[...] + acc_ref[...].T, g2_ref[...], b2_ref[...])


def _peer_mix(h, u, vt, s1, e1, s2, e2, tau, ln2_g, ln2_b, alpha, tm, n_i):
    m, d = h.shape
    n_heads, _, n_keys, _ = s2.shape
    assert m % tm == 0 and tm % LANES == 0 and n_keys % n_i == 0 and n_i % 8 == 0
    tn = n_i * n_keys
    n_c = tm // LANES
    row_stat = pl.BlockSpec((n_heads, n_c, n_i, LANES), lambda i, j: (0, i, j, 0))
    once = pl.Buffered(1)
    full_stat = pl.BlockSpec((n_heads, n_c, n_keys, LANES), lambda i, j: (0, i, 0, 0), pipeline_mode=once)
    return pl.pallas_call(
        functools.partial(_peer_mix_kernel, alpha, n_i),
        grid=(m // tm, n_keys // n_i),
        in_specs=[pl.BlockSpec((tm, d), lambda i, j: (i, 0), pipeline_mode=once),
                  pl.BlockSpec((tn, d), lambda i, j: (j, 0)),
                  pl.BlockSpec((d, tn), lambda i, j: (0, j)),
                  row_stat, row_stat, full_stat, full_stat,
                  pl.BlockSpec((n_c, n_heads, LANES), lambda i, j: (i, 0, 0)),
                  pl.BlockSpec((1, d), lambda i, j: (0, 0)),
                  pl.BlockSpec((1, d), lambda i, j: (0, 0))],
        out_specs=pl.BlockSpec((tm, d), lambda i, j: (i, 0)),
        out_shape=jax.ShapeDtypeStruct((m, d), F32),
        scratch_shapes=[pltpu.VMEM((tm, d), BF16), pltpu.VMEM((n_c, tn, LANES), F32),
                        pltpu.VMEM((n_c, tn, LANES), BF16), pltpu.VMEM((d, tm), F32)],
        compiler_params=_params("parallel", "arbitrary"),
        name="peer_mix",
    )(h, u, vt, s1, e1, s2, e2, tau, ln2_g, ln2_b)


def _pick_tile(m, target):
    t = min(m, target)
    while m % t:
        t //= 2
    return t


def _layer_weights(w_in, mu_shift, w0, w_decay_up, a0, w_aaa_up, w_gate_up, k_k, k_a, r_k,
                   lnx_g, lnx_b, w_out, ln1_g, ln1_b, peer_wq, peer_subkeys, peer_u, peer_v,
                   ln2_g, ln2_b, sb_width):
    rw_width = w0.shape[0]
    d_lora, a_lora, g_lora = w_decay_up.shape[0], w_aaa_up.shape[0], w_gate_up.shape[0]
    in_cols = w_in.shape[1]
    rw_cols = in_cols - 3 * sb_width
    tail = rw_cols - 3 * rw_width
    assert tail == d_lora + a_lora + g_lora
    tail_w = -(-tail // LANES) * LANES
    cols_pad = 3 * sb_width + 3 * rw_width + tail_w
    assert sb_width % rw_width == 0 and (3 * sb_width + 3 * rw_width) % tail_w == 0
    row = lambda x: x.reshape(1, -1).astype(F32)
    mu = jnp.pad(mu_shift, (0, tail_w - tail))

    def lora(w, start):
        return jnp.zeros((tail_w, rw_width), F32).at[start:start + w.shape[0]].set(w).astype(BF16)

    n_ph, _, n_keys, dk = peer_subkeys.shape
    return dict(
        w_in=jnp.pad(w_in, ((0, 0), (0, cols_pad - in_cols))).astype(BF16),
        rw_prm=[row(mu[:rw_width]), row(mu[rw_width:2 * rw_width]), row(mu[2 * rw_width:3 * rw_width]),
                row(mu[3 * rw_width:]), row(w0), row(a0), row(k_k), row(k_a), row(r_k),
                lora(w_decay_up, 0), lora(w_aaa_up, d_lora), lora(w_gate_up, d_lora + a_lora),
                _head_ones()],
        lnx_g=row(lnx_g), lnx_b=row(lnx_b), w_out=w_out.astype(BF16),
        ln1_g=row(ln1_g), ln1_b=row(ln1_b), ln2_g=row(ln2_g), ln2_b=row(ln2_b),
        peer_wq=peer_wq.astype(BF16),
        subkeys=peer_subkeys.reshape(n_ph * 2, n_keys, dk).astype(BF16),
        peer_u=peer_u.astype(BF16), peer_vt=peer_v.astype(BF16).T,
        rw_width=rw_width, rw_cols=rw_cols, tail_w=tail_w, cols_pad=cols_pad,
    )


def _in_proj(x, lw):
    m = x.shape[0]
    n = lw["cols_pad"]
    tn = n // 3 if n % (3 * LANES) == 0 else n
    return _matmul(x, lw["w_in"], _pick_tile(m, 512), tn, "in_proj")


def _peer(h, lw, alpha):
    m = h.shape[0]
    m_pad = -(-m // LANES) * LANES
    hp = jnp.pad(h, ((0, m_pad - m), (0, 0))) if m_pad != m else h
    q = _matmul(hp, lw["peer_wq"], _pick_tile(m_pad, 512), lw["peer_wq"].shape[1], "peer_q")
    s1, e1, s2, e2, tau = _peer_stats(q, lw["subkeys"], _pick_tile(m_pad, 256))
    y = _peer_mix(hp, lw["peer_u"], lw["peer_vt"], s1, e1, s2, e2, tau, lw["ln2_g"], lw["ln2_b"],
                  alpha, _pick_tile(m_pad, 512), 8)
    return y[:m]


def _wkv_to_lanes(s):
    b, h, v, k = s.shape
    return s.transpose(0, 2, 1, 3).reshape(b, v, h * k)


def _wkv_from_lanes(s, n_heads):
    b, v, hk = s.shape
    return s.reshape(b, v, n_heads, hk // n_heads).transpose(0, 2, 1, 3)


def kernel(x_prompt, x_sample, cache_sb_k, cache_sb_v, state_wkv, state_shift, page_table, w_in, sb_bias, mu_shift, w0, w_decay_up, a0, w_aaa_up, w_gate_up, k_k, k_a, r_k, lnx_g, lnx_b, w_out, ln1_g, ln1_b, peer_wq, peer_subkeys, peer_u, peer_v, ln2_g, ln2_b):
    depth = w_in.shape[0]
    bsz, seq, d_model = x_prompt.shape
    dec_b, dec_seq, _ = x_sample.shape
    assert dec_seq == 1, "one new token per sampled sequence"
    _, n_pool, page, sb_heads, head_dim = cache_sb_k.shape
    rw_heads = state_wkv.shape[2]
    assert head_dim == HEAD_DIM and state_wkv.shape[3] == HEAD_DIM
    sb_width = sb_heads * HEAD_DIM
    alpha = (2 * depth) ** 0.25

    h_p = [x_prompt[b] for b in range(bsz)]
    h_s = x_sample[:, 0]
    outs = {name: [] for name in ("kp", "vp", "wp", "sp", "ks", "vs", "ws", "ss")}
    for l in range(depth):
        lw = _layer_weights(w_in[l], mu_shift[l], w0[l], w_decay_up[l], a0[l], w_aaa_up[l], w_gate_up[l],
                            k_k[l], k_a[l], r_k[l].reshape(-1), lnx_g[l], lnx_b[l], w_out[l], ln1_g[l],
                            ln1_b[l], peer_wq[l], peer_subkeys[l], peer_u[l], peer_v[l], ln2_g[l],
                            ln2_b[l], sb_width)
        rw_width, rw_cols, tail_w = lw["rw_width"], lw["rw_cols"], lw["tail_w"]
        rw0 = 3 * sb_width

        p_s = _in_proj(h_s, lw)
        pool_t = lambda c: c.transpose(0, 2, 3, 1).reshape(n_pool, sb_width, page)
        n_pp = _pick_tile(page_table.shape[1], 16)
        tok_args = (p_s[:, :sb_width], sb_bias[l], pool_t(cache_sb_k[l]), pool_t(cache_sb_v[l]), page_table, n_pp)
        tok_steps = dec_b * (page_table.shape[1] // n_pp)
        sb_s = None

        kp, vp, wp, sp = [], [], [], []
        for b in range(bsz):
            x = h_p[b]
            p = _in_proj(x, lw)
            kp.append(p[:, sb_width:2 * sb_width].reshape(seq, sb_heads, HEAD_DIM))
            vp.append(p[:, 2 * sb_width:3 * sb_width].reshape(seq, sb_heads, HEAD_DIM))
            sp.append(p[seq - 1:seq, rw0:rw0 + rw_cols])
            zeros = [jnp.zeros((1, rw_width), F32)] * 3 + [jnp.zeros((1, tail_w), F32)]
            r, w, k, v, a, bb, gate, bonus = _rw_prep(p, None, zeros, lw["rw_prm"], _pick_tile(seq, 256),
                                                      rw_width, rw0, tail_w)
            seqs = [u[None] for u in (r, w, k, v, a, bb)]
            s0 = jnp.zeros((1, HEAD_DIM, rw_width), F32)
            if sb_s is None and seq % tok_steps == 0 and (seq // tok_steps) % 8 == 0:
                sb_s, y, s_fin = _sb_tok(*tok_args, scan=(seqs, s0))
            else:
                y, s_fin = _rw_scan(seqs, s0, _pick_tile(seq, 128))
            wp.append(_wkv_from_lanes(s_fin, rw_heads)[0])
            tq = _pick_tile(seq, 512)
            sb = _sb_seq(p, sb_bias[l], sb_heads, tq, _pick_tile(tq, 256))
            h = _mix(y[0], bonus, gate, sb, x, lw["w_out"], lw["lnx_g"], lw["lnx_b"], lw["ln1_g"],
                     lw["ln1_b"], alpha, _pick_tile(seq, 256))
            h_p[b] = _peer(h, lw, alpha)
        outs["kp"].append(jnp.stack(kp)); outs["vp"].append(jnp.stack(vp))
        outs["wp"].append(jnp.stack(wp)); outs["sp"].append(jnp.stack(sp))

        p = p_s
        outs["ks"].append(p[:, sb_width:2 * sb_width].reshape(dec_b, 1, sb_heads, HEAD_DIM))
        outs["vs"].append(p[:, 2 * sb_width:3 * sb_width].reshape(dec_b, 1, sb_heads, HEAD_DIM))
        outs["ss"].append(p[:, None, rw0:rw0 + rw_cols])
        shift = state_shift[l][:, 0]
        prev = [shift[:, :rw_width], shift[:, rw_width:2 * rw_width], shift[:, 2 * rw_width:3 * rw_width],
                jnp.pad(shift[:, 3 * rw_width:], ((0, 0), (0, tail_w - (rw_cols - 3 * rw_width))))]
        r, w, k, v, a, bb, gate, bonus = _rw_prep(p, prev, None, lw["rw_prm"], _pick_tile(dec_b, 256),
                                                  rw_width, rw0, tail_w)
        y, s_fin = _rw_scan([u[:, None] for u in (r, w, k, v, a, bb)], _wkv_to_lanes(state_wkv[l]), 1)
        outs["ws"].append(_wkv_from_lanes(s_fin, rw_heads))
        if sb_s is None:
            sb_s = _sb_tok(*tok_args)
        h = _mix(y[:, 0], bonus, gate, sb_s.reshape(dec_b, sb_width), h_s, lw["w_out"], lw["lnx_g"], lw["lnx_b"],
                 lw["ln1_g"], lw["ln1_b"], alpha, _pick_tile(dec_b, 256))
        h_s = _peer(h, lw, alpha)

    stack = lambda name: jnp.stack(outs[name])
    return (jnp.stack(h_p), h_s[:, None], stack("kp"), stack("vp"), stack("wp"), stack("sp"),
            stack("ks"), stack("vs"), stack("ws"), stack("ss"))
```
